```python
import math
import jax, jax.numpy as jnp
from jax import lax
import numpy as np

D_MODEL = 4096
BATCH = 4
SEQ = 2048
DEPTH = 1
DEC_BATCH = 128
DEC_SEQ = 4
PAST_LEN = 16384
PAGE_SIZE = 128

MLA_HEADS = 16
MLA_NOPE = 128
MLA_ROPE = 64
MLA_V = 128
Q_RANK = 768
KV_RANK = 512
ROPE_THETA = 10000.0
LATENT_DIM = KV_RANK + MLA_ROPE
MLA_SCALE = (MLA_NOPE + MLA_ROPE) ** -0.5
DSA_HEADS = 16
DSA_DIM = 128
DSA_SCALE = DSA_DIM ** -0.5
IDX_HEADS = 32
IDX_DIM = 128
IDX_SCALE = IDX_DIM ** -0.5
IDX_TOPK_MAX = 256
NUM_BUCKETS = 32
MAX_DISTANCE = 128
N_GROUPS = 8
EXPERTS_PER_GROUP = 8
N_EXPERTS = N_GROUPS * EXPERTS_PER_GROUP
TOP_K_IN_GROUP = 2
D_EXPERT = 1024
Q_BLOCK = 128
LN_EPS = 1e-5
RMS_EPS = 1e-6
DEEPNORM_ALPHA = (2.0 * DEPTH) ** 0.25
DEEPNORM_BETA = (8.0 * DEPTH) ** -0.25
IN_SPLITS = (Q_RANK, KV_RANK, MLA_ROPE, DSA_HEADS * DSA_DIM, DSA_DIM, DSA_DIM, IDX_HEADS * IDX_DIM, IDX_DIM, IDX_HEADS)
D_IN = sum(IN_SPLITS)
MIX_WIDTH = MLA_HEADS * MLA_V + DSA_HEADS * DSA_DIM

kernel_name = "hymba_mla_dsa_hiermoe_deepnorm_step"


def _rmsnorm(x, g):
    xf = x.astype(jnp.float32)
    y = xf * lax.rsqrt(jnp.mean(xf * xf, axis=-1, keepdims=True) + RMS_EPS)
    return (y * g.astype(jnp.float32)).astype(x.dtype)


def _layernorm(x, g, b):
    xf = x.astype(jnp.float32)
    mu = jnp.mean(xf, axis=-1, keepdims=True)
    var = jnp.mean(jnp.square(xf - mu), axis=-1, keepdims=True)
    y = (xf - mu) * lax.rsqrt(var + LN_EPS)
    return (y * g.astype(jnp.float32) + b.astype(jnp.float32)).astype(x.dtype)


def _rope(x, pos):
    half = MLA_ROPE // 2
    inv = ROPE_THETA ** (-jnp.arange(half, dtype=jnp.float32) / half)
    ang = pos.astype(jnp.float32)[:, None] * inv[None, :]
    cos = jnp.cos(ang)[:, None, :]
    sin = jnp.sin(ang)[:, None, :]
    xf = x.astype(jnp.float32)
    x1, x2 = xf[..., :half], xf[..., half:]
    return jnp.concatenate([x1 * cos - x2 * sin, x1 * sin + x2 * cos], axis=-1).astype(x.dtype)


def _t5_bucket(dist):
    dist = jnp.maximum(dist, 0)
    max_exact = NUM_BUCKETS // 2
    far = max_exact + (jnp.log(jnp.maximum(dist, 1).astype(jnp.float32) / max_exact)
                       / math.log(MAX_DISTANCE / max_exact) * (NUM_BUCKETS - max_exact)).astype(jnp.int32)
    return jnp.where(dist < max_exact, dist, jnp.minimum(far, NUM_BUCKETS - 1))


def _project(x, pos, w_in, g_q, g_kv, w_uq, w_uk):
    B, T, _ = x.shape
    h = jnp.einsum('btd,de->bte', x, w_in)
    offs = np.cumsum(IN_SPLITS)[:-1].tolist()
    c_q, c_kv, k_pe, q_b, k_b, v_b, q_idx, k_idx, w_idx = jnp.split(h, offs, axis=-1)
    q = jnp.einsum('btr,re->bte', _rmsnorm(c_q, g_q), w_uq).reshape(B, T, MLA_HEADS, MLA_NOPE + MLA_ROPE)
    q_pe = _rope(q[..., MLA_NOPE:], pos)
    q_lat = jnp.einsum('bthn,chn->bthc', q[..., :MLA_NOPE], w_uk)
    q_cat = jnp.concatenate([q_lat, q_pe], axis=-1)
    k_pe = _rope(k_pe[:, :, None, :], pos)[:, :, 0, :]
    latent = jnp.concatenate([_rmsnorm(c_kv, g_kv), k_pe], axis=-1)
    kv = jnp.concatenate([k_b, v_b], axis=-1)
    q_b = q_b.reshape(B, T, DSA_HEADS, DSA_DIM)
    q_idx = q_idx.reshape(B, T, IDX_HEADS, IDX_DIM)
    return q_cat, q_b, q_idx, w_idx, latent, kv, k_idx


def _mla_attend(q_cat, latent, q_pos, k_pos):
    s = jnp.einsum('thc,sc->hts', q_cat, latent).astype(jnp.float32) * MLA_SCALE
    s = jnp.where((k_pos[None, :] <= q_pos[:, None])[None], s, -jnp.inf)
    c_kv = latent[:, :KV_RANK]
    p = jax.nn.softmax(s, axis=-1).astype(c_kv.dtype)
    return jnp.einsum('hts,sc->thc', p, c_kv)


def _dsa_attend(q_b, q_idx, w_idx, kv, k_idx, rel_bias, q_pos, k_pos, top_k):
    dots = jax.nn.relu(jnp.einsum('thd,sd->ths', q_idx, k_idx).astype(jnp.float32) * IDX_SCALE)
    score = jnp.einsum('ths,th->ts', dots, w_idx.astype(jnp.float32) * (IDX_HEADS ** -0.5))
    score = jnp.where(k_pos[None, :] <= q_pos[:, None], score, -jnp.inf)
    _, sel = lax.top_k(score, top_k)
    sel_pos = k_pos[sel]
    valid = sel_pos <= q_pos[:, None]
    kv_sel = kv[sel]
    k_sel, v_sel = kv_sel[..., :DSA_DIM], kv_sel[..., DSA_DIM:]
    logits = jnp.einsum('thd,tkd->htk', q_b, k_sel).astype(jnp.float32) * DSA_SCALE
    bias = rel_bias[_t5_bucket(q_pos[:, None] - sel_pos)]
    logits = logits + jnp.transpose(bias, (2, 0, 1)).astype(jnp.float32)
    logits = jnp.where(valid[None], logits, -jnp.inf)
    p = jax.nn.softmax(logits, axis=-1).astype(v_sel.dtype)
    return jnp.einsum('htk,tkd->thd', p, v_sel)


def _merge(o_lat, o_b, w_uv, w_o):
    B, T = o_lat.shape[:2]
    o_a = jnp.einsum('bthc,chv->bthv', o_lat, w_uv)
    o = jnp.concatenate([o_a.reshape(B, T, -1), o_b.reshape(B, T, -1)], axis=-1)
    return jnp.einsum('btm,md->btd', o, w_o)


def _prompt_mix(x, rel_bias, w_in, g_q, g_kv, w_uq, w_uk, w_uv, w_o):
    B, T, _ = x.shape
    pos = jnp.arange(T, dtype=jnp.int32)
    q_cat, q_b, q_idx, w_idx, latent, kv, k_idx = _project(x, pos, w_in, g_q, g_kv, w_uq, w_uk)
    top_k = min(IDX_TOPK_MAX, T // 4)
    nb = T // Q_BLOCK

    def to_blocks(a):
        return jnp.swapaxes(a.reshape((B, nb, Q_BLOCK) + a.shape[2:]), 0, 1)

    def from_blocks(a):
        return jnp.swapaxes(a, 0, 1).reshape((B, T) + a.shape[3:])

    def block(args):
        qc, qb, qi, wi, qp = args

        def one(qc1, qb1, qi1, wi1, lat1, kv1, ki1):
            return (_mla_attend(qc1, lat1, qp, pos),
                    _dsa_attend(qb1, qi1, wi1, kv1, ki1, rel_bias, qp, pos, top_k))
        return jax.vmap(one)(qc, qb, qi, wi, latent, kv, k_idx)

    o_a, o_b = lax.map(block, (to_blocks(q_cat), to_blocks(q_b), to_blocks(q_idx), to_blocks(w_idx),
                               pos.reshape(nb, Q_BLOCK)))
    return _merge(from_blocks(o_a), from_blocks(o_b), w_uv, w_o), latent, kv, k_idx


def _sample_mix(x, layer, cache_latent, cache_kv, cache_idx, page_table, rel_bias,
                w_in, g_q, g_kv, w_uq, w_uk, w_uv, w_o):
    B, T, _ = x.shape
    past = page_table.shape[1] * PAGE_SIZE
    q_pos = past + jnp.arange(T, dtype=jnp.int32)
    k_pos = jnp.arange(past + T, dtype=jnp.int32)
    q_cat, q_b, q_idx, w_idx, latent, kv, k_idx = _project(x, q_pos, w_in, g_q, g_kv, w_uq, w_uk)
    top_k = min(IDX_TOPK_MAX, (past + T) // 4)

    def rows(cache, pt, new):
        return jnp.concatenate([cache[layer, pt].reshape(past, cache.shape[-1]), new], axis=0)

    def seq(args):
        pt, qc, qb, qi, wi, lat_new, kv_new, ki_new = args
        o_a = _mla_attend(qc, rows(cache_latent, pt, lat_new), q_pos, k_pos)
        o_b = _dsa_attend(qb, qi, wi, rows(cache_kv, pt, kv_new), rows(cache_idx, pt, ki_new),
                          rel_bias, q_pos, k_pos, top_k)
        return o_a, o_b

    o_a, o_b = lax.map(seq, (page_table, q_cat, q_b, q_idx, w_idx, latent, kv, k_idx))
    return _merge(o_a, o_b, w_uv, w_o), latent, kv, k_idx


def _hier_moe(x, layer, w_group, b_group, w_router, b_router, w_gate, w_up, w_down):
    shp = x.shape
    xt = x.reshape(-1, shp[-1])
    T = xt.shape[0]
    g_prob = jax.nn.softmax(jnp.einsum('td,dg->tg', xt, w_group).astype(jnp.float32)
                            + b_group.astype(jnp.float32), axis=-1)
    grp = jnp.argmax(g_prob, axis=-1).astype(jnp.int32)
    p_grp = jnp.take_along_axis(g_prob, grp[:, None], axis=-1)
    e_logits = (jnp.einsum('td,de->te', xt, w_router).astype(jnp.float32)
                + b_router.astype(jnp.float32)).reshape(T, N_GROUPS, EXPERTS_PER_GROUP)
    e_logits = jnp.take_along_axis(e_logits, grp[:, None, None], axis=1)[:, 0]
    top_p, top_i = lax.top_k(jax.nn.softmax(e_logits, axis=-1), TOP_K_IN_GROUP)
    gate = p_grp * top_p / jnp.sum(top_p, axis=-1, keepdims=True)
    expert = grp[:, None] * EXPERTS_PER_GROUP + top_i.astype(jnp.int32)
    A = T * TOP_K_IN_GROUP
    blk = min(128, max(8, A // N_EXPERTS))
    n_blk = -(-(A + N_EXPERTS * (blk - 1)) // blk)
    P = n_blk * blk
    e_flat = expert.reshape(A)
    tok_flat = jnp.repeat(jnp.arange(T, dtype=jnp.int32), TOP_K_IN_GROUP)
    order = jnp.argsort(e_flat)
    e_sorted = e_flat[order]
    counts = jnp.bincount(e_flat, length=N_EXPERTS).astype(jnp.int32)
    padded = (counts + blk - 1) // blk * blk
    pad_end = jnp.cumsum(padded)
    pad_start = pad_end - padded
    start = jnp.cumsum(counts) - counts
    dest = pad_start[e_sorted] + jnp.arange(A, dtype=jnp.int32) - start[e_sorted]
    tok_buf = jnp.full((P,), T, jnp.int32).at[dest].set(tok_flat[order])
    gate_buf = jnp.zeros((P,), jnp.float32).at[dest].set(gate.reshape(A)[order])
    blk_expert = jnp.minimum(jnp.searchsorted(pad_end, jnp.arange(n_blk, dtype=jnp.int32) * blk, side='right'),
                             N_EXPERTS - 1)
    x_pad = jnp.concatenate([xt, jnp.zeros((1, shp[-1]), xt.dtype)], axis=0)

    def expert_block(args):
        tok, e = args
        xb = x_pad[tok]
        hb = jax.nn.silu(xb @ w_gate[layer, e]) * (xb @ w_up[layer, e])
        return hb @ w_down[layer, e]

    out = lax.map(expert_block, (tok_buf.reshape(n_blk, blk), blk_expert))
    y = jnp.zeros((T + 1, shp[-1]), jnp.float32).at[tok_buf].add(
        out.reshape(P, shp[-1]).astype(jnp.float32) * gate_buf[:, None])
    return y[:T].astype(x.dtype).reshape(shp)


def setup_inputs(seed: int = 0) -> dict:
    key = jax.random.key(seed)
    ks = jax.random.split(key, 26)
    n_pages = PAST_LEN // PAGE_SIZE
    n_pool = (DEC_BATCH * n_pages * 5) // 4

    def nrm(k, shape, scale):
        return jax.random.normal(k, shape, jnp.float32) * scale

    page_table = jax.random.permutation(ks[5], n_pool)[:DEC_BATCH * n_pages].reshape(DEC_BATCH, n_pages).astype(jnp.int32)
    return {
        "x_prompt": nrm(ks[0], (BATCH, SEQ, D_MODEL), 1.0),
        "x_sample": nrm(ks[1], (DEC_BATCH, DEC_SEQ, D_MODEL), 1.0),
        "cache_latent": nrm(ks[2], (DEPTH, n_pool, PAGE_SIZE, LATENT_DIM), 1.0),
        "cache_kv": nrm(ks[3], (DEPTH, n_pool, PAGE_SIZE, 2 * DSA_DIM), 1.0),
        "cache_idx": nrm(ks[4], (DEPTH, n_pool, PAGE_SIZE, IDX_DIM), 1.0),
        "page_table": page_table,
        "rel_bias": nrm(ks[6], (NUM_BUCKETS, DSA_HEADS), 0.5),
        "w_in": nrm(ks[7], (DEPTH, D_MODEL, D_IN), D_MODEL ** -0.5),
        "g_q": 1.0 + nrm(ks[8], (DEPTH, Q_RANK), 0.02),
        "g_kv": 1.0 + nrm(ks[9], (DEPTH, KV_RANK), 0.02),
        "w_uq": nrm(ks[10], (DEPTH, Q_RANK, MLA_HEADS * (MLA_NOPE + MLA_ROPE)), Q_RANK ** -0.5),
        "w_uk": nrm(ks[11], (DEPTH, KV_RANK, MLA_HEADS, MLA_NOPE), KV_RANK ** -0.5),
        "w_uv": nrm(ks[12], (DEPTH, KV_RANK, MLA_HEADS, MLA_V), KV_RANK ** -0.5),
        "w_o": nrm(ks[13], (DEPTH, MIX_WIDTH, D_MODEL), MIX_WIDTH ** -0.5 * DEEPNORM_BETA),
        "ln1_g": 1.0 + nrm(ks[14], (DEPTH, D_MODEL), 0.02),
        "ln1_b": nrm(ks[15], (DEPTH, D_MODEL), 0.02),
        "w_group": nrm(ks[16], (DEPTH, D_MODEL, N_GROUPS), D_MODEL ** -0.5),
        "b_group": nrm(ks[17], (DEPTH, N_GROUPS), 0.01),
        "w_router": nrm(ks[18], (DEPTH, D_MODEL, N_EXPERTS), D_MODEL ** -0.5),
        "b_router": nrm(ks[19], (DEPTH, N_EXPERTS), 0.01),
        "w_gate": nrm(ks[20], (DEPTH, N_EXPERTS, D_MODEL, D_EXPERT), D_MODEL ** -0.5),
        "w_up": nrm(ks[21], (DEPTH, N_EXPERTS, D_MODEL, D_EXPERT), D_MODEL ** -0.5),
        "w_down": nrm(ks[22], (DEPTH, N_EXPERTS, D_EXPERT, D_MODEL), D_EXPERT ** -0.5 * DEEPNORM_BETA),
        "ln2_g": 1.0 + nrm(ks[23], (DEPTH, D_MODEL), 0.02),
        "ln2_b": nrm(ks[24], (DEPTH, D_MODEL), 0.02),
    }


def reference(x_prompt, x_sample, cache_latent, cache_kv, cache_idx, page_table, rel_bias,
              w_in, g_q, g_kv, w_uq, w_uk, w_uv, w_o, ln1_g, ln1_b,
              w_group, b_group, w_router, b_router, w_gate, w_up, w_down, ln2_g, ln2_b):
    xp, xs = x_prompt, x_sample
    lat_p, kv_p, idx_p, lat_s, kv_s, idx_s = [], [], [], [], [], []
    for l in range(DEPTH):
        attn_w = (w_in[l], g_q[l], g_kv[l], w_uq[l], w_uk[l], w_uv[l], w_o[l])
        moe_w = (w_group[l], b_group[l], w_router[l], b_router[l], w_gate, w_up, w_down)
        mix_p, lp, kp, ip = _prompt_mix(xp, rel_bias, *attn_w)
        mix_s, ls, ksm, ism = _sample_mix(xs, l, cache_latent, cache_kv, cache_idx, page_table, rel_bias, *attn_w)
        hp = _layernorm(DEEPNORM_ALPHA * xp + mix_p, ln1_g[l], ln1_b[l])
        hs = _layernorm(DEEPNORM_ALPHA * xs + mix_s, ln1_g[l], ln1_b[l])
        xp = _layernorm(DEEPNORM_ALPHA * hp + _hier_moe(hp, l, *moe_w), ln2_g[l], ln2_b[l])
        xs = _layernorm(DEEPNORM_ALPHA * hs + _hier_moe(hs, l, *moe_w), ln2_g[l], ln2_b[l])
        lat_p.append(lp); kv_p.append(kp); idx_p.append(ip)
        lat_s.append(ls); kv_s.append(ksm); idx_s.append(ism)
    return (xp, xs, jnp.stack(lat_p), jnp.stack(kv_p), jnp.stack(idx_p),
            jnp.stack(lat_s), jnp.stack(kv_s), jnp.stack(idx_s))
```

```python
import functools
import math

import jax
import jax.numpy as jnp
from jax import lax
from jax.experimental import pallas as pl
from jax.experimental.pallas import tpu as pltpu

BF = jnp.bfloat16
F32 = jnp.float32
I32 = jnp.int32

MLA_HEADS = 16
MLA_NOPE = 128
MLA_ROPE = 64
MLA_V = 128
Q_RANK = 768
KV_RANK = 512
LATENT_DIM = KV_RANK + MLA_ROPE
ROPE_THETA = 10000.0
MLA_SCALE = (MLA_NOPE + MLA_ROPE) ** -0.5
DSA_HEADS = 16
DSA_DIM = 128
DSA_SCALE = DSA_DIM ** -0.5
IDX_HEADS = 32
IDX_DIM = 128
IDX_SCALE = IDX_DIM ** -0.5
IDX_TOPK_MAX = 256
NUM_BUCKETS = 32
MAX_DISTANCE = 128
N_GROUPS = 8
EXPERTS_PER_GROUP = 8
N_EXPERTS = N_GROUPS * EXPERTS_PER_GROUP
PAGE_SIZE = 128
LN_EPS = 1e-5
RMS_EPS = 1e-6

LANE = 128
QBLK = 128
NEG = -1e30
VMEM_LIMIT = 56 * 1024 * 1024

C_QI = 0
C_QB = C_QI + IDX_HEADS * IDX_DIM
C_CKV = C_QB + DSA_HEADS * DSA_DIM
C_KV = C_CKV + KV_RANK
C_CQ = C_KV + 2 * DSA_DIM
C_KI = C_CQ + Q_RANK
C_KPE = C_KI + IDX_DIM
C_WI = C_KPE + 2 * MLA_ROPE
D_PROJ = C_WI + LANE
assert C_QB % (DSA_HEADS * DSA_DIM) == 0 and C_CKV % KV_RANK == 0 and C_KV % (2 * DSA_DIM) == 0
assert C_CQ % Q_RANK == 0 and C_KI % LANE == 0


def _cparams(n_axes, vmem=VMEM_LIMIT):
    return pltpu.CompilerParams(dimension_semantics=("arbitrary",) * n_axes, vmem_limit_bytes=vmem)


def _pick(n, cands):
    for c in cands:
        if n % c == 0:
            return c
    return n


def _nt_dot(a, b):
    return lax.dot_general(a, b, (((1,), (1,)), ((), ())), preferred_element_type=F32)


def _mm_kernel(x_ref, w_ref, o_ref, xb_ref):
    @pl.when(pl.program_id(1) == 0)
    def _():
        xb_ref[...] = x_ref[...].astype(BF)

    o_ref[...] = jnp.dot(xb_ref[...], w_ref[...], preferred_element_type=F32).astype(o_ref.dtype)


def _matmul(x, w, out_dtype):
    m, k = x.shape
    n = w.shape[1]
    tm = _pick(m, (512, 256, 128))
    tn = _pick(n, (1152, 1024, 896, 512, 384, 256, 128))
    return pl.pallas_call(
        _mm_kernel,
        grid=(m // tm, n // tn),
        in_specs=[pl.BlockSpec((tm, k), lambda i, j: (i, 0)),
                  pl.BlockSpec((k, tn), lambda i, j: (0, j))],
        out_specs=pl.BlockSpec((tm, tn), lambda i, j: (i, j)),
        out_shape=jax.ShapeDtypeStruct((m, n), out_dtype),
        scratch_shapes=[pltpu.VMEM((tm, k), BF)],
        compiler_params=_cparams(2),
        name="in_proj",
    )(x, w)


def _qpath_kernel(cq_ref, g_ref, wn_ref, wp_ref, wuk_ref, cs_ref, o_ref):
    x = cq_ref[...]
    y = x * lax.rsqrt(jnp.mean(x * x, axis=-1, keepdims=True) + RMS_EPS) * g_ref[...]
    yb = y.astype(BF)
    a = jnp.dot(yb, wn_ref[...], preferred_element_type=F32)
    b = jnp.dot(yb, wp_ref[...], preferred_element_type=F32)
    cs = cs_ref[...]
    for h in range(MLA_HEADS):
        qn = a[:, h * LANE:(h + 1) * LANE].astype(BF)
        ql = jnp.dot(qn, wuk_ref[h], preferred_element_type=F32) * MLA_SCALE
        pe = b[:, h * LANE:(h + 1) * LANE] * cs
        pe = (pe + pltpu.roll(pe, MLA_ROPE, 1)) * MLA_SCALE
        o_ref[0, h, :, 0:KV_RANK] = ql.astype(BF)
        o_ref[0, h, :, KV_RANK:LATENT_DIM] = pe[:, 0:MLA_ROPE].astype(BF)


def _qpath(hproj, g_q, w_nope, w_pr, w_ukt, cs):
    tt = hproj.shape[0]
    nb = tt // QBLK
    hw = MLA_HEADS * LANE
    return pl.pallas_call(
        _qpath_kernel,
        grid=(nb,),
        in_specs=[pl.BlockSpec((QBLK, Q_RANK), lambda i: (i, C_CQ // Q_RANK)),
                  pl.BlockSpec((1, Q_RANK), lambda i: (0, 0)),
                  pl.BlockSpec((Q_RANK, hw), lambda i: (0, 0)),
                  pl.BlockSpec((Q_RANK, hw), lambda i: (0, 0)),
                  pl.BlockSpec((MLA_HEADS, MLA_NOPE, KV_RANK), lambda i: (0, 0, 0)),
                  pl.BlockSpec((QBLK, LANE), lambda i: (i, 0))],
        out_specs=pl.BlockSpec((1, MLA_HEADS, QBLK, LATENT_DIM), lambda i: (i, 0, 0, 0)),
        out_shape=jax.ShapeDtypeStruct((nb, MLA_HEADS, QBLK, LATENT_DIM), BF),
        compiler_params=_cparams(1),
        name="mla_qpath",
    )(hproj, g_q, w_nope, w_pr, w_ukt, cs)


def _kvpath_kernel(ckv_ref, kpe_ref, kv_ref, ki_ref, g_ref, cs_ref,
                   lat_ref, kvo_ref, kio_ref, latb_ref, kvb_ref, kib_ref):
    x = ckv_ref[...]
    c = x * lax.rsqrt(jnp.mean(x * x, axis=-1, keepdims=True) + RMS_EPS) * g_ref[...]
    pe = kpe_ref[...] * cs_ref[...]
    pe = (pe + pltpu.roll(pe, MLA_ROPE, 1))[:, 0:MLA_ROPE]
    lat_ref[:, 0:KV_RANK] = c
    lat_ref[:, KV_RANK:LATENT_DIM] = pe
    latb_ref[:, 0:KV_RANK] = c.astype(BF)
    latb_ref[:, KV_RANK:LATENT_DIM] = pe.astype(BF)
    kv = kv_ref[...]
    kvo_ref[...] = kv
    kvb_ref[...] = kv.astype(BF)
    ki = ki_ref[...]
    kio_ref[...] = ki
    kib_ref[...] = ki.astype(BF)


def _kvpath(hproj, g_kv, cs):
    tt = hproj.shape[0]
    tm = _pick(tt, (256, 128))
    kvw = 2 * DSA_DIM
    outs = [jax.ShapeDtypeStruct((tt, LATENT_DIM), F32), jax.ShapeDtypeStruct((tt, kvw), F32),
            jax.ShapeDtypeStruct((tt, IDX_DIM), F32), jax.ShapeDtypeStruct((tt, LATENT_DIM), BF),
            jax.ShapeDtypeStruct((tt, kvw), BF), jax.ShapeDtypeStruct((tt, IDX_DIM), BF)]
    ospec = [pl.BlockSpec((tm, LATENT_DIM), lambda i: (i, 0)), pl.BlockSpec((tm, kvw), lambda i: (i, 0)),
             pl.BlockSpec((tm, IDX_DIM), lambda i: (i, 0))]
    return pl.pallas_call(
        _kvpath_kernel,
        grid=(tt // tm,),
        in_specs=[pl.BlockSpec((tm, KV_RANK), lambda i: (i, C_CKV // KV_RANK)),
                  pl.BlockSpec((tm, LANE), lambda i: (i, C_KPE // LANE)),
                  pl.BlockSpec((tm, kvw), lambda i: (i, C_KV // kvw)),
                  pl.BlockSpec((tm, IDX_DIM), lambda i: (i, C_KI // IDX_DIM)),
                  pl.BlockSpec((1, KV_RANK), lambda i: (0, 0)),
                  pl.BlockSpec((tm, LANE), lambda i: (i, 0))],
        out_specs=ospec + ospec,
        out_shape=outs,
        compiler_params=_cparams(1),
        name="kv_path",
    )(hproj, hproj, hproj, hproj, g_kv, cs)


def _mla_prompt_kernel(q_ref, lat_ref, o_ref, m_ref, l_ref, acc_ref, *, kc):
    i = pl.program_id(1)
    rows = MLA_HEADS * QBLK
    q = q_ref[0].reshape(rows, LATENT_DIM)
    m_ref[...] = jnp.full(m_ref.shape, NEG, F32)
    l_ref[...] = jnp.zeros(l_ref.shape, F32)
    acc_ref[...] = jnp.zeros(acc_ref.shape, F32)
    row_t = i * QBLK + (lax.broadcasted_iota(I32, (rows, kc), 0) & (QBLK - 1))
    col = lax.broadcasted_iota(I32, (rows, kc), 1)
    n_chunks = ((i + 1) * QBLK + kc - 1) // kc

    def body(c, carry):
        off = pl.multiple_of(c * kc, kc)
        k = lat_ref[0, pl.ds(off, kc), :]
        s = _nt_dot(q, k)
        s = jnp.where(col + off <= row_t, s, NEG)
        m_old = m_ref[...]
        m_new = jnp.maximum(m_old, jnp.max(s, axis=-1, keepdims=True))
        a = jnp.exp(m_old - m_new)
        p = jnp.exp(s - m_new)
        l_ref[...] = a * l_ref[...] + jnp.sum(p, axis=-1, keepdims=True)
        acc_ref[...] = a * acc_ref[...] + jnp.dot(p.astype(BF), k[:, 0:KV_RANK], preferred_element_type=F32)
        m_ref[...] = m_new
        return carry

    lax.fori_loop(0, n_chunks, body, 0)
    o = acc_ref[...] / l_ref[...]
    o_ref[0] = o.reshape(MLA_HEADS, QBLK, KV_RANK).astype(o_ref.dtype)


def _mla_prompt(q_cat, lat_b):
    nbatch, seq, _ = lat_b.shape
    nbq = seq // QBLK
    kc = _pick(seq, (256, 128))
    rows = MLA_HEADS * QBLK
    return pl.pallas_call(
        functools.partial(_mla_prompt_kernel, kc=kc),
        grid=(nbatch, nbq),
        in_specs=[pl.BlockSpec((1, MLA_HEADS, QBLK, LATENT_DIM), lambda b, i: (b * nbq + i, 0, 0, 0)),
                  pl.BlockSpec((1, seq, LATENT_DIM), lambda b, i: (b, 0, 0))],
        out_specs=pl.BlockSpec((1, MLA_HEADS, QBLK, KV_RANK), lambda b, i: (b * nbq + i, 0, 0, 0)),
        out_shape=jax.ShapeDtypeStruct((nbatch * nbq, MLA_HEADS, QBLK, KV_RANK), BF),
        scratch_shapes=[pltpu.VMEM((rows, 1), F32), pltpu.VMEM((rows, 1), F32), pltpu.VMEM((rows, KV_RANK), F32)],
        compiler_params=_cparams(2),
        name="mla_prompt",
    )(q_cat, lat_b)


def _uv_kernel(o_ref, w_ref, out_ref):
    for h in range(MLA_HEADS):
        out_ref[:, h * MLA_V:(h + 1) * MLA_V] = jnp.dot(
            o_ref[0, h], w_ref[h], preferred_element_type=F32).astype(out_ref.dtype)


def _uv(o_lat, w_uvt):
    nb = o_lat.shape[0]
    return pl.pallas_call(
        _uv_kernel,
        grid=(nb,),
        in_specs=[pl.BlockSpec((1, MLA_HEADS, QBLK, KV_RANK), lambda i: (i, 0, 0, 0)),
                  pl.BlockSpec((MLA_HEADS, KV_RANK, MLA_V), lambda i: (0, 0, 0))],
        out_specs=pl.BlockSpec((QBLK, MLA_HEADS * MLA_V), lambda i: (i, 0)),
        out_shape=jax.ShapeDtypeStruct((nb * QBLK, MLA_HEADS * MLA_V), BF),
        compiler_params=_cparams(1),
        name="mla_uv",
    )(o_lat, w_uvt)


def _float_key(x):
    bits = pltpu.bitcast(x, I32)
    return jnp.where(bits < 0, bits ^ jnp.int32(0x7FFFFFFF), bits)


def _topk_select(key_ref, col, k, idx_bits):
    nrows = key_ref.shape[0]
    kf = jnp.float32(k)

    def count(pred):
        return jnp.sum(jnp.where(pred, 1.0, 0.0), axis=-1, keepdims=True)

    def vbody(b, thr):
        cand = thr + lax.shift_left(jnp.int32(1), 31 - b)
        return jnp.where(count(key_ref[...] >= cand) >= kf, cand, thr)

    thr = lax.fori_loop(0, 32, vbody, jnp.full((nrows, 1), -2 ** 31, I32))
    key = key_ref[...]
    need = kf - count(key > thr)

    def cbody(b, cut):
        cand = cut + lax.shift_left(jnp.int32(1), idx_bits - 1 - b)
        n = jnp.sum(jnp.where(key_ref[...] == thr, jnp.where(col < cand, 1.0, 0.0), 0.0), axis=-1, keepdims=True)
        return jnp.where(n <= need, cand, cut)

    cut = lax.fori_loop(0, idx_bits, cbody, jnp.zeros((nrows, 1), I32))
    return (key > thr) | ((key == thr) & (col < cut))


def _dsa_prompt_kernel(qi_ref, wi_ref, qb_ref, ki_ref, kv_ref, bt_ref, o_ref, sc_ref, key_ref, lg_ref, *, topk):
    i = pl.program_id(1)
    seq = ki_ref.shape[1]
    w = wi_ref[...] * (IDX_SCALE * IDX_HEADS ** -0.5)
    lane = lax.broadcasted_iota(I32, (QBLK, LANE), 1)
    sc_ref[...] = jnp.zeros(sc_ref.shape, F32)

    def hbody(hh, carry):
        q = qi_ref[:, pl.ds(pl.multiple_of(hh * IDX_DIM, IDX_DIM), IDX_DIM)].astype(BF)
        d = _nt_dot(q, ki_ref[0])
        wc = jnp.sum(jnp.where(lane == hh, w, 0.0), axis=-1, keepdims=True)
        sc_ref[...] += jnp.maximum(d, 0.0) * wc
        return carry

    lax.fori_loop(0, IDX_HEADS, hbody, 0)

    t_idx = i * QBLK + lax.broadcasted_iota(I32, (QBLK, seq), 0)
    s_idx = lax.broadcasted_iota(I32, (QBLK, seq), 1)
    causal = s_idx <= t_idx
    key_ref[...] = _float_key(jnp.where(causal, sc_ref[...], -jnp.inf))
    sel = _topk_select(key_ref, s_idx, topk, int(math.ceil(math.log2(seq))) + 1)
    sc_ref[...] = jnp.where(sel & causal, 0.0, NEG)

    off = pl.multiple_of(i * QBLK, QBLK)

    def abody(h, carry):
        hoff = pl.multiple_of(h * DSA_DIM, DSA_DIM)
        q = qb_ref[:, pl.ds(hoff, DSA_DIM)].astype(BF)
        lg_ref[...] = _nt_dot(q, kv_ref[0, :, 0:DSA_DIM]) * DSA_SCALE + sc_ref[...]
        lg_ref[:, pl.ds(off, QBLK)] += bt_ref[h, :, QBLK:2 * QBLK]

        @pl.when(i >= 1)
        def _():
            lg_ref[:, pl.ds(off - QBLK, QBLK)] += bt_ref[h, :, 0:QBLK]

        lg = lg_ref[...]
        m = jnp.max(lg, axis=-1, keepdims=True)
        p = jnp.exp(lg - m)
        l = jnp.sum(p, axis=-1, keepdims=True)
        o = jnp.dot(p.astype(BF), kv_ref[0, :, DSA_DIM:2 * DSA_DIM], preferred_element_type=F32) / l
        o_ref[:, pl.ds(hoff, DSA_DIM)] = o.astype(o_ref.dtype)
        return carry

    lax.fori_loop(0, DSA_HEADS, abody, 0)


def _dsa_prompt(hproj, ki_b, kv_b, bt, nbatch, seq, topk):
    nbq = seq // QBLK
    qiw = IDX_HEADS * IDX_DIM
    qbw = DSA_HEADS * DSA_DIM
    return pl.pallas_call(
        functools.partial(_dsa_prompt_kernel, topk=topk),
        grid=(nbatch, nbq),
        in_specs=[pl.BlockSpec((QBLK, qiw), lambda b, i: (b * nbq + i, C_QI // qiw)),
                  pl.BlockSpec((QBLK, LANE), lambda b, i: (b * nbq + i, C_WI // LANE)),
                  pl.BlockSpec((QBLK, qbw), lambda b, i: (b * nbq + i, C_QB // qbw)),
                  pl.BlockSpec((1, seq, IDX_DIM), lambda b, i: (b, 0, 0)),
                  pl.BlockSpec((1, seq, 2 * DSA_DIM), lambda b, i: (b, 0, 0)),
                  pl.BlockSpec((DSA_HEADS, QBLK, 2 * QBLK), lambda b, i: (0, 0, 0))],
        out_specs=pl.BlockSpec((QBLK, qbw), lambda b, i: (b * nbq + i, 0)),
        out_shape=jax.ShapeDtypeStruct((nbatch * seq, qbw), BF),
        scratch_shapes=[pltpu.VMEM((QBLK, seq), F32), pltpu.VMEM((QBLK, seq), I32), pltpu.VMEM((QBLK, seq), F32)],
        compiler_params=_cparams(2),
        name="dsa_prompt",
    )(hproj, hproj, hproj, ki_b, kv_b, bt)


def _page_specs(n, width, pg):
    def spec(j):
        return pl.BlockSpec((None, None, PAGE_SIZE, width), lambda s, c, pt: (0, pt[s, c * pg + j], 0, 0))
    return [spec(j) for j in range(n)]


def _group_scores(d, wcol, ntok):
    x = jnp.maximum(d, 0.0) * wcol
    return jnp.concatenate(
        [jnp.sum(x[t * IDX_HEADS:(t + 1) * IDX_HEADS], axis=0, keepdims=True) for t in range(ntok)], axis=0)


def _idx_sample_kernel(pt_ref, q_ref, w_ref, *refs, pg, ntok):
    pages = refs[:pg]
    knew_ref, o_ref, onew_ref = refs[pg:]
    c = pl.program_id(1)
    q = q_ref[0]
    wcol = w_ref[0] * (IDX_SCALE * IDX_HEADS ** -0.5)
    kc = jnp.concatenate([p[...].astype(BF) for p in pages], axis=0)
    s = _group_scores(_nt_dot(q, kc), wcol, ntok)
    pad = 8 - ntok
    o_ref[0] = jnp.concatenate([s, s[0:pad]], axis=0)

    @pl.when(c == pl.num_programs(1) - 1)
    def _():
        sn = _group_scores(_nt_dot(q, knew_ref[0]), wcol, ntok)
        sn = jnp.concatenate([sn, sn[0:pad]], axis=0)
        row = lax.broadcasted_iota(I32, (8, LANE), 0) % ntok
        colv = lax.broadcasted_iota(I32, (8, LANE), 1)
        onew_ref[0] = jnp.where(colv <= row, sn, -jnp.inf)


def _idx_sample(page_table, q_idx, w_col, cache_idx, k_new, pg, ntok):
    nseq, npages = page_table.shape
    nchunks = npages // pg
    sc = pg * PAGE_SIZE
    rows = ntok * IDX_HEADS
    grid_spec = pltpu.PrefetchScalarGridSpec(
        num_scalar_prefetch=1,
        grid=(nseq, nchunks),
        in_specs=[pl.BlockSpec((1, rows, IDX_DIM), lambda s, c, pt: (s, 0, 0)),
                  pl.BlockSpec((1, rows, 1), lambda s, c, pt: (s, 0, 0))]
        + _page_specs(pg, IDX_DIM, pg)
        + [pl.BlockSpec((1, LANE, IDX_DIM), lambda s, c, pt: (s, 0, 0))],
        out_specs=[pl.BlockSpec((1, 8, sc), lambda s, c, pt: (s, 0, c)),
                   pl.BlockSpec((1, 8, LANE), lambda s, c, pt: (s, 0, 0))],
    )
    return pl.pallas_call(
        functools.partial(_idx_sample_kernel, pg=pg, ntok=ntok),
        grid_spec=grid_spec,
        out_shape=[jax.ShapeDtypeStruct((nseq, 8, npages * PAGE_SIZE), F32),
                   jax.ShapeDtypeStruct((nseq, 8, LANE), F32)],
        compiler_params=_cparams(2),
        name="idx_sample",
    )(page_table, q_idx, w_col, *([cache_idx] * pg), k_new)


def _topk_sample_kernel(sp_ref, sn_ref, mp_ref, mn_ref, key_ref, *, topk):
    past = sp_ref.shape[2]
    total = past + LANE
    score = jnp.concatenate([sp_ref[0], sn_ref[0]], axis=1)
    key_ref[...] = _float_key(score)
    col = lax.broadcasted_iota(I32, (8, total), 1)
    sel = _topk_select(key_ref, col, topk, int(math.ceil(math.log2(total))) + 1)
    mask = jnp.where(sel & (score > -jnp.inf), 0.0, NEG)
    mp_ref[0] = mask[:, 0:past]
    mn_ref[0] = mask[:, past:total]


def _topk_sample(s_past, s_new, topk):
    nseq, _, past = s_past.shape
    return pl.pallas_call(
        functools.partial(_topk_sample_kernel, topk=topk),
        grid=(nseq,),
        in_specs=[pl.BlockSpec((1, 8, past), lambda s: (s, 0, 0)),
                  pl.BlockSpec((1, 8, LANE), lambda s: (s, 0, 0))],
        out_specs=[pl.BlockSpec((1, 8, past), lambda s: (s, 0, 0)),
                   pl.BlockSpec((1, 8, LANE), lambda s: (s, 0, 0))],
        out_shape=[jax.ShapeDtypeStruct((nseq, 8, past), F32), jax.ShapeDtypeStruct((nseq, 8, LANE), F32)],
        scratch_shapes=[pltpu.VMEM((8, past + LANE), I32)],
        compiler_params=_cparams(1),
        name="topk_sample",
    )(s_past, s_new)


def _flash_init(m_ref, l_ref, acc_ref):
    m_ref[...] = jnp.full(m_ref.shape, NEG, F32)
    l_ref[...] = jnp.zeros(l_ref.shape, F32)
    acc_ref[...] = jnp.zeros(acc_ref.shape, F32)


def _flash_step(s, v, m_ref, l_ref, acc_ref):
    m_old = m_ref[...]
    m_new = jnp.maximum(m_old, jnp.max(s, axis=-1, keepdims=True))
    a = jnp.exp(m_old - m_new)
    p = jnp.exp(s - m_new)
    l_ref[...] = a * l_ref[...] + jnp.sum(p, axis=-1, keepdims=True)
    acc_ref[...] = a * acc_ref[...] + jnp.dot(p.astype(BF), v, preferred_element_type=F32)
    m_ref[...] = m_new


def _mla_sample_kernel(pt_ref, q_ref, *refs, pg, nheads):
    pages = refs[:pg]
    new_ref, o_ref, m_ref, l_ref, acc_ref = refs[pg:]
    c = pl.program_id(1)

    @pl.when(c == 0)
    def _():
        _flash_init(m_ref, l_ref, acc_ref)

    q = q_ref[0]
    k = jnp.concatenate([p[...].astype(BF) for p in pages], axis=0)
    _flash_step(_nt_dot(q, k), k[:, 0:KV_RANK], m_ref, l_ref, acc_ref)

    @pl.when(c == pl.num_programs(1) - 1)
    def _():
        kn = new_ref[0]
        rows = q.shape[0]
        s = _nt_dot(q, kn)
        t = lax.broadcasted_iota(I32, (rows, LANE), 0) // nheads
        colv = lax.broadcasted_iota(I32, (rows, LANE), 1)
        _flash_step(jnp.where(colv <= t, s, NEG), kn[:, 0:KV_RANK], m_ref, l_ref, acc_ref)
        o_ref[0] = (acc_ref[...] / l_ref[...]).astype(o_ref.dtype)


def _mla_sample(page_table, q_cat, cache_latent, lat_new, pg):
    nseq, npages = page_table.shape
    rows = q_cat.shape[1]
    grid_spec = pltpu.PrefetchScalarGridSpec(
        num_scalar_prefetch=1,
        grid=(nseq, npages // pg),
        in_specs=[pl.BlockSpec((1, rows, LATENT_DIM), lambda s, c, pt: (s, 0, 0))]
        + _page_specs(pg, LATENT_DIM, pg)
        + [pl.BlockSpec((1, LANE, LATENT_DIM), lambda s, c, pt: (s, 0, 0))],
        out_specs=pl.BlockSpec((1, rows, KV_RANK), lambda s, c, pt: (s, 0, 0)),
        scratch_shapes=[pltpu.VMEM((rows, 1), F32), pltpu.VMEM((rows, 1), F32), pltpu.VMEM((rows, KV_RANK), F32)],
    )
    return pl.pallas_call(
        functools.partial(_mla_sample_kernel, pg=pg, nheads=MLA_HEADS),
        grid_spec=grid_spec,
        out_shape=jax.ShapeDtypeStruct((nseq, rows, KV_RANK), BF),
        compiler_params=_cparams(2),
        name="mla_sample",
    )(page_table, q_cat, *([cache_latent] * pg), lat_new)


def _expand_rows(mask, ntok, nheads):
    return jnp.concatenate(
        [jnp.broadcast_to(mask[t:t + 1], (nheads, mask.shape[1])) for t in range(ntok)], axis=0)


def _dsa_sample_kernel(pt_ref, q_ref, mp_ref, mn_ref, *refs, pg, ntok):
    pages = refs[:pg]
    new_ref, blast_ref, bnew_ref, o_ref, m_ref, l_ref, acc_ref = refs[pg:]
    c = pl.program_id(1)
    last = pl.num_programs(1) - 1

    @pl.when(c == 0)
    def _():
        _flash_init(m_ref, l_ref, acc_ref)

    q = q_ref[0]
    kv = jnp.concatenate([p[...].astype(BF) for p in pages], axis=0)
    kb = kv[:, 0:DSA_DIM]
    vb = kv[:, DSA_DIM:2 * DSA_DIM]
    lg = _nt_dot(q, kb) * DSA_SCALE + _expand_rows(mp_ref[0], ntok, DSA_HEADS)

    @pl.when(c < last)
    def _():
        _flash_step(lg, vb, m_ref, l_ref, acc_ref)

    @pl.when(c == last)
    def _():
        sc = lg.shape[1]
        near = lg[:, sc - LANE:sc] + blast_ref[...]
        lg2 = near if sc == LANE else jnp.concatenate([lg[:, 0:sc - LANE], near], axis=1)
        _flash_step(lg2, vb, m_ref, l_ref, acc_ref)
        kvn = new_ref[0]
        ln = (_nt_dot(q, kvn[:, 0:DSA_DIM]) * DSA_SCALE + bnew_ref[...]
              + _expand_rows(mn_ref[0], ntok, DSA_HEADS))
        _flash_step(ln, kvn[:, DSA_DIM:2 * DSA_DIM], m_ref, l_ref, acc_ref)
        o_ref[0] = (acc_ref[...] / l_ref[...]).astype(o_ref.dtype)


def _dsa_sample(page_table, q_b, m_past, m_new, cache_kv, kv_new, b_last, b_new, pg, ntok):
    nseq, npages = page_table.shape
    rows = q_b.shape[1]
    sc = pg * PAGE_SIZE
    grid_spec = pltpu.PrefetchScalarGridSpec(
        num_scalar_prefetch=1,
        grid=(nseq, npages // pg),
        in_specs=[pl.BlockSpec((1, rows, DSA_DIM), lambda s, c, pt: (s, 0, 0)),
                  pl.BlockSpec((1, 8, sc), lambda s, c, pt: (s, 0, c)),
                  pl.BlockSpec((1, 8, LANE), lambda s, c, pt: (s, 0, 0))]
        + _page_specs(pg, 2 * DSA_DIM, pg)
        + [pl.BlockSpec((1, LANE, 2 * DSA_DIM), lambda s, c, pt: (s, 0, 0)),
           pl.BlockSpec((rows, LANE), lambda s, c, pt: (0, 0)),
           pl.BlockSpec((rows, LANE), lambda s, c, pt: (0, 0))],
        out_specs=pl.BlockSpec((1, rows, DSA_DIM), lambda s, c, pt: (s, 0, 0)),
        scratch_shapes=[pltpu.VMEM((rows, 1), F32), pltpu.VMEM((rows, 1), F32), pltpu.VMEM((rows, DSA_DIM), F32)],
    )
    return pl.pallas_call(
        functools.partial(_dsa_sample_kernel, pg=pg, ntok=ntok),
        grid_spec=grid_spec,
        out_shape=jax.ShapeDtypeStruct((nseq, rows, DSA_DIM), BF),
        compiler_params=_cparams(2),
        name="dsa_sample",
    )(page_table, q_b, m_past, m_new, *([cache_kv] * pg), kv_new, b_last, b_new)


def _layernorm(z, g, b):
    mu = jnp.mean(z, axis=-1, keepdims=True)
    zc = z - mu
    var = jnp.mean(zc * zc, axis=-1, keepdims=True)
    return zc * lax.rsqrt(var + LN_EPS) * g + b


def _wo_ln_kernel(oa_ref, ob_ref, w_ref, x_ref, g_ref, b_ref, o_ref, *, nka, alpha):
    k = pl.program_id(1)

    @pl.when(k == 0)
    def _():
        o_ref[...] = jnp.zeros(o_ref.shape, F32)

    @pl.when(k < nka)
    def _():
        o_ref[...] += jnp.dot(oa_ref[...], w_ref[...], preferred_element_type=F32)

    @pl.when(k >= nka)
    def _():
        o_ref[...] += jnp.dot(ob_ref[...], w_ref[...], preferred_element_type=F32)

    @pl.when(k == pl.num_programs(1) - 1)
    def _():
        o_ref[...] = _layernorm(alpha * x_ref[...] + o_ref[...], g_ref[...], b_ref[...])


def _wo_ln(o_a, o_b, w_o, x, g, b, alpha):
    tt, d = x.shape
    wa = o_a.shape[1]
    tm = _pick(tt, (512, 256, 128))
    tk = _pick(wa, (512, 256, 128))
    nka = wa // tk
    nk = nka + o_b.shape[1] // tk
    return pl.pallas_call(
        functools.partial(_wo_ln_kernel, nka=nka, alpha=alpha),
        grid=(tt // tm, nk),
        in_specs=[pl.BlockSpec((tm, tk), lambda i, k: (i, jnp.minimum(k, nka - 1))),
                  pl.BlockSpec((tm, tk), lambda i, k: (i, jnp.maximum(k - nka, 0))),
                  pl.BlockSpec((tk, d), lambda i, k: (k, 0)),
                  pl.BlockSpec((tm, d), lambda i, k: (i, 0)),
                  pl.BlockSpec((1, d), lambda i, k: (0, 0)),
                  pl.BlockSpec((1, d), lambda i, k: (0, 0))],
        out_specs=pl.BlockSpec((tm, d), lambda i, k: (i, 0)),
        out_shape=jax.ShapeDtypeStruct((tt, d), F32),
        compiler_params=_cparams(2),
        name="wo_ln1",
    )(o_a, o_b, w_o, x, g, b)


def _router_kernel(h_ref, whi_ref, wlo_ref, b_ref, oi_ref, of_ref):
    h = h_ref[...]
    hhi = h.astype(BF)
    hlo = (h - hhi.astype(F32)).astype(BF)
    whi = whi_ref[...]
    logits = (jnp.dot(hhi, whi, preferred_element_type=F32)
              + jnp.dot(hhi, wlo_ref[...], preferred_element_type=F32)
              + jnp.dot(hlo, whi, preferred_element_type=F32)) + b_ref[...]
    lane = lax.broadcasted_iota(I32, logits.shape, 1)
    big = jnp.int32(2 ** 30)
    is_g = lane < N_GROUPS
    lgp = jnp.where(is_g, logits, -jnp.inf)
    mg = jnp.max(lgp, axis=-1, keepdims=True)
    p_grp = 1.0 / jnp.sum(jnp.exp(lgp - mg), axis=-1, keepdims=True)
    grp = jnp.min(jnp.where(lgp == mg, lane, big), axis=-1, keepdims=True)
    lo = N_GROUPS + grp * EXPERTS_PER_GROUP
    in_g = (lane >= lo) & (lane < lo + EXPERTS_PER_GROUP)
    le = jnp.where(in_g, logits, -jnp.inf)
    m1 = jnp.max(le, axis=-1, keepdims=True)
    i1 = jnp.min(jnp.where(le == m1, lane, big), axis=-1, keepdims=True)
    le2 = jnp.where(lane == i1, -jnp.inf, le)
    m2 = jnp.max(le2, axis=-1, keepdims=True)
    i2 = jnp.min(jnp.where(le2 == m2, lane, big), axis=-1, keepdims=True)
    e2 = jnp.exp(m2 - m1)
    g1 = p_grp / (1.0 + e2)
    g2 = p_grp * e2 / (1.0 + e2)
    oi_ref[...] = jnp.where(lane == 0, i1 - N_GROUPS, jnp.where(lane == 1, i2 - N_GROUPS, 0))
    of_ref[...] = jnp.where(lane == 0, g1, jnp.where(lane == 1, g2, 0.0))


def _router(h, w_hi, w_lo, bias):
    tt, d = h.shape
    tm = _pick(tt, (512, 256, 128))
    return pl.pallas_call(
        _router_kernel,
        grid=(tt // tm,),
        in_specs=[pl.BlockSpec((tm, d), lambda i: (i, 0)),
                  pl.BlockSpec((d, LANE), lambda i: (0, 0)),
                  pl.BlockSpec((d, LANE), lambda i: (0, 0)),
                  pl.BlockSpec((1, LANE), lambda i: (0, 0))],
        out_specs=[pl.BlockSpec((tm, LANE), lambda i: (i, 0)), pl.BlockSpec((tm, LANE), lambda i: (i, 0))],
        out_shape=[jax.ShapeDtypeStruct((tt, LANE), I32), jax.ShapeDtypeStruct((tt, LANE), F32)],
        compiler_params=_cparams(1),
        name="moe_router",
    )(h, w_hi, w_lo, bias)


MOE_SUB = 128


def _experts_kernel(bexp_ref, brows_ref, bout_ref, tok_ref, h_hbm, wg_ref, wu_ref, wd_ref, o_ref,
                    xf_ref, xb_ref, wgb_ref, wub_ref, wdb_ref, sem, *, bm):
    j = pl.program_id(0)
    f = pl.program_id(1)
    rows = brows_ref[j]
    nsub = bm // MOE_SUB

    def row_copy(r):
        tok = tok_ref[j * bm + r]
        return pltpu.make_async_copy(h_hbm.at[pl.ds(tok, 1)], xf_ref.at[pl.ds(r, 1)], sem)

    @pl.when(rows > 0)
    def _():
        n_gather = ((rows + MOE_SUB - 1) // MOE_SUB) * MOE_SUB

        @pl.when(f == 0)
        def _():
            def issue(r, carry):
                row_copy(r).start()
                return carry

            lax.fori_loop(0, n_gather, issue, 0)

            def wait(r, carry):
                row_copy(r).wait()
                return carry

            lax.fori_loop(0, n_gather, wait, 0)
            for sb in range(nsub):
                @pl.when(sb * MOE_SUB < rows)
                def _():
                    xb_ref[sb * MOE_SUB:(sb + 1) * MOE_SUB, :] = xf_ref[sb * MOE_SUB:(sb + 1) * MOE_SUB, :].astype(BF)

        wgb_ref[...] = wg_ref[...].astype(BF)
        wub_ref[...] = wu_ref[...].astype(BF)
        wdb_ref[...] = wd_ref[...].astype(BF)
        for sb in range(nsub):
            @pl.when(sb * MOE_SUB < rows)
            def _():
                sl = slice(sb * MOE_SUB, (sb + 1) * MOE_SUB)
                x = xb_ref[sl, :]
                g = jnp.dot(x, wgb_ref[...], preferred_element_type=F32)
                u = jnp.dot(x, wub_ref[...], preferred_element_type=F32)
                hb = (g * jax.nn.sigmoid(g) * u).astype(BF)
                y = jnp.dot(hb, wdb_ref[...], preferred_element_type=F32)

                @pl.when(f == 0)
                def _():
                    o_ref[sl, :] = y

                @pl.when(f > 0)
                def _():
                    o_ref[sl, :] += y


def _experts(plan, h, w_gate, w_up, w_down, bm, nblk):
    tt, d = h.shape
    de = w_gate.shape[-1]
    tf = _pick(de, (256, 128))
    grid_spec = pltpu.PrefetchScalarGridSpec(
        num_scalar_prefetch=4,
        grid=(nblk, de // tf),
        in_specs=[pl.BlockSpec(memory_space=pl.ANY),
                  pl.BlockSpec((None, None, d, tf), lambda j, f, be, br, bo, tk: (0, be[j], 0, f)),
                  pl.BlockSpec((None, None, d, tf), lambda j, f, be, br, bo, tk: (0, be[j], 0, f)),
                  pl.BlockSpec((None, None, tf, d), lambda j, f, be, br, bo, tk: (0, be[j], f, 0))],
        out_specs=pl.BlockSpec((bm, d), lambda j, f, be, br, bo, tk: (bo[j], 0)),
        scratch_shapes=[pltpu.VMEM((bm, d), F32), pltpu.VMEM((bm, d), BF),
                        pltpu.VMEM((d, tf), BF), pltpu.VMEM((d, tf), BF), pltpu.VMEM((tf, d), BF),
                        pltpu.SemaphoreType.DMA(())],
    )
    return pl.pallas_call(
        functools.partial(_experts_kernel, bm=bm),
        grid_spec=grid_spec,
        out_shape=jax.ShapeDtypeStruct((nblk * bm, d), F32),
        compiler_params=_cparams(2),
        name="moe_experts",
    )(plan["b_exp"], plan["b_rows"], plan["b_out"], plan["tok_buf"], h, w_gate, w_up, w_down)


def _moe_plan(e1, e2, bm, nblk):
    tt = e1.shape[0]
    na = 2 * tt
    e_flat = jnp.stack([e1, e2], axis=1).reshape(na)
    order = jnp.argsort(e_flat, stable=True).astype(I32)
    e_sorted = e_flat[order]
    counts = jnp.bincount(e_flat, length=N_EXPERTS).astype(I32)
    nb_e = (counts + bm - 1) // bm
    blk_end = jnp.cumsum(nb_e)
    blk_start = blk_end - nb_e
    start = jnp.cumsum(counts) - counts
    dest = blk_start[e_sorted] * bm + jnp.arange(na, dtype=I32) - start[e_sorted]
    tok_buf = jnp.zeros((nblk * bm,), I32).at[dest].set(order // 2)
    pos = jnp.zeros((na,), I32).at[order].set(dest)
    j = jnp.arange(nblk, dtype=I32)
    n_valid = blk_end[-1]
    jj = jnp.minimum(j, n_valid - 1)
    b_exp = jnp.minimum(jnp.searchsorted(blk_end, jj, side="right").astype(I32), N_EXPERTS - 1)
    b_rows = jnp.where(j < n_valid, jnp.clip(counts[b_exp] - (jj - blk_start[b_exp]) * bm, 0, bm), 0)
    return {"tok_buf": tok_buf, "pos": pos, "b_exp": b_exp, "b_rows": b_rows.astype(I32), "b_out": jj}


def _combine_ln_kernel(pos_ref, y_hbm, gate_ref, h_ref, g_ref, b_ref, o_ref, r1_ref, r2_ref, sem, *, tm, alpha):
    i = pl.program_id(0)

    def copies(r):
        a = 2 * (i * tm + r)
        return (pltpu.make_async_copy(y_hbm.at[pl.ds(pos_ref[a], 1)], r1_ref.at[pl.ds(r, 1)], sem),
                pltpu.make_async_copy(y_hbm.at[pl.ds(pos_ref[a + 1], 1)], r2_ref.at[pl.ds(r, 1)], sem))

    def issue(r, carry):
        c1, c2 = copies(r)
        c1.start()
        c2.start()
        return carry

    lax.fori_loop(0, tm, issue, 0)

    def wait(r, carry):
        c1, c2 = copies(r)
        c1.wait()
        c2.wait()
        return carry

    lax.fori_loop(0, tm, wait, 0)
    gate = gate_ref[...]
    y = gate[:, 0:1] * r1_ref[...] + gate[:, 1:2] * r2_ref[...]
    o_ref[...] = _layernorm(alpha * h_ref[...] + y, g_ref[...], b_ref[...])


def _combine_ln(pos, y_sorted, gates, h, g, b, alpha):
    tt, d = h.shape
    tm = _pick(tt, (128,))
    grid_spec = pltpu.PrefetchScalarGridSpec(
        num_scalar_prefetch=1,
        grid=(tt // tm,),
        in_specs=[pl.BlockSpec(memory_space=pl.ANY),
                  pl.BlockSpec((tm, LANE), lambda i, p: (i, 0)),
                  pl.BlockSpec((tm, d), lambda i, p: (i, 0)),
                  pl.BlockSpec((1, d), lambda i, p: (0, 0)),
                  pl.BlockSpec((1, d), lambda i, p: (0, 0))],
        out_specs=pl.BlockSpec((tm, d), lambda i, p: (i, 0)),
        scratch_shapes=[pltpu.VMEM((tm, d), F32), pltpu.VMEM((tm, d), F32), pltpu.SemaphoreType.DMA(())],
    )
    return pl.pallas_call(
        functools.partial(_combine_ln_kernel, tm=tm, alpha=alpha),
        grid_spec=grid_spec,
        out_shape=jax.ShapeDtypeStruct((tt, d), F32),
        compiler_params=_cparams(1),
        name="moe_combine_ln2",
    )(pos, y_sorted, gates, h, g, b)


def _rotate_half_cols(w):
    half = MLA_ROPE // 2
    return jnp.concatenate([-w[..., half:], w[..., :half]], axis=-1)


def _prep_w_in(w_in):
    offs = [0]
    for s in (Q_RANK, KV_RANK, MLA_ROPE, DSA_HEADS * DSA_DIM, DSA_DIM, DSA_DIM, IDX_HEADS * IDX_DIM, IDX_DIM, IDX_HEADS):
        offs.append(offs[-1] + s)
    c_q, c_kv, k_pe, q_b, k_b, v_b, q_i, k_i, w_i = [w_in[:, offs[n]:offs[n + 1]] for n in range(9)]
    pad = jnp.zeros((w_in.shape[0], LANE - IDX_HEADS), w_in.dtype)
    return jnp.concatenate([q_i, q_b, c_kv, k_b, v_b, c_q, k_i, k_pe, _rotate_half_cols(k_pe), w_i, pad],
                           axis=1).astype(BF)


def _prep_w_uq(w_uq):
    w = w_uq.reshape(Q_RANK, MLA_HEADS, MLA_NOPE + MLA_ROPE)
    w_nope = w[:, :, :MLA_NOPE].reshape(Q_RANK, MLA_HEADS * MLA_NOPE)
    pe = w[:, :, MLA_NOPE:]
    w_pr = jnp.concatenate([pe, _rotate_half_cols(pe)], axis=-1).reshape(Q_RANK, MLA_HEADS * 2 * MLA_ROPE)
    return w_nope.astype(BF), w_pr.astype(BF)


def _rope_table(pos):
    half = MLA_ROPE // 2
    inv = ROPE_THETA ** (-jnp.arange(half, dtype=F32) / half)
    ang = pos.astype(F32)[:, None] * inv[None, :]
    cos, sin = jnp.cos(ang), jnp.sin(ang)
    return jnp.concatenate([cos, cos, sin, sin], axis=-1)


def _bias_by_distance(rel_bias, n):
    dist = jnp.arange(n, dtype=I32)
    max_exact = NUM_BUCKETS // 2
    far = max_exact + (jnp.log(jnp.maximum(dist, 1).astype(F32) / max_exact)
                       / math.log(MAX_DISTANCE / max_exact) * (NUM_BUCKETS - max_exact)).astype(I32)
    bucket = jnp.where(dist < max_exact, dist, jnp.minimum(far, NUM_BUCKETS - 1))
    return rel_bias[bucket]


def kernel(x_prompt, x_sample, cache_latent, cache_kv, cache_idx, page_table, rel_bias, w_in, g_q, g_kv, w_uq, w_uk, w_uv, w_o, ln1_g, ln1_b, w_group, b_group, w_router, b_router, w_gate, w_up, w_down, ln2_g, ln2_b):
    depth = w_in.shape[0]
    assert depth == 1
    nbatch, seq, d_model = x_prompt.shape
    nseq, ntok, _ = x_sample.shape
    npages = page_table.shape[1]
    past = npages * PAGE_SIZE
    tp = nbatch * seq
    ts = nseq * ntok
    alpha = (2.0 * depth) ** 0.25
    assert seq % QBLK == 0 and ts % QBLK == 0 and QBLK % ntok == 0 and 4 <= ntok <= 8

    w_in_p = _prep_w_in(w_in[0])
    w_nope, w_pr = _prep_w_uq(w_uq[0])
    w_ukt = jnp.transpose(w_uk[0], (1, 2, 0)).astype(BF)
    w_uvt = jnp.transpose(w_uv[0], (1, 0, 2)).astype(BF)
    w_o_b = w_o[0].astype(BF)
    w_rt = jnp.concatenate([w_group[0], w_router[0],
                            jnp.zeros((d_model, LANE - N_GROUPS - N_EXPERTS), F32)], axis=1)
    w_rt_hi = w_rt.astype(BF)
    w_rt_lo = (w_rt - w_rt_hi.astype(F32)).astype(BF)
    b_rt = jnp.concatenate([b_group[0], b_router[0], jnp.zeros((LANE - N_GROUPS - N_EXPERTS,), F32)])[None, :]

    pos_p = jnp.tile(jnp.arange(seq, dtype=I32), nbatch)
    pos_s = jnp.tile(past + jnp.arange(ntok, dtype=I32), nseq)
    cs = _rope_table(jnp.concatenate([pos_p, pos_s]))

    tab = _bias_by_distance(rel_bias, 2 * QBLK)
    far = rel_bias[NUM_BUCKETS - 1]
    tabc = (tab - far[None, :]).astype(F32)
    r = jnp.arange(QBLK)[:, None]
    cc = jnp.arange(2 * QBLK)[None, :]
    bt = jnp.transpose(tabc[jnp.clip(QBLK + r - cc, 0, 2 * QBLK - 1)], (2, 0, 1))
    tq = jnp.repeat(jnp.arange(ntok), DSA_HEADS)[:, None]
    hq = jnp.tile(jnp.arange(DSA_HEADS), ntok)[:, None]
    cl = jnp.arange(LANE)[None, :]
    b_last = tabc[jnp.clip(PAGE_SIZE + tq - cl, 0, 2 * QBLK - 1), hq]
    b_new = jnp.where(cl <= tq, tabc[jnp.clip(tq - cl, 0, 2 * QBLK - 1), hq], 0.0)

    x_all = jnp.concatenate([x_prompt.reshape(tp, d_model), x_sample.reshape(ts, d_model)], axis=0)
    hproj = _matmul(x_all, w_in_p, F32)
    q_cat = _qpath(hproj, g_q, w_nope, w_pr, w_ukt, cs)
    lat, kvo, kio, lat_b, kv_b, ki_b = _kvpath(hproj, g_kv, cs)

    nbp = tp // QBLK
    o_lat_p = _mla_prompt(q_cat[:nbp], lat_b[:tp].reshape(nbatch, seq, LATENT_DIM))
    topk_p = min(IDX_TOPK_MAX, seq // 4)
    o_b_p = _dsa_prompt(hproj, ki_b[:tp].reshape(nbatch, seq, IDX_DIM), kv_b[:tp].reshape(nbatch, seq, 2 * DSA_DIM),
                        bt, nbatch, seq, topk_p)

    pg = _pick(npages, (16, 8, 4, 2, 1))
    nbs = ts // QBLK
    spb = QBLK // ntok
    q_cat_s = q_cat[nbp:].reshape(nbs, MLA_HEADS, spb, ntok, LATENT_DIM)
    q_cat_s = jnp.transpose(q_cat_s, (0, 2, 3, 1, 4)).reshape(nseq, ntok * MLA_HEADS, LATENT_DIM)

    def new_rows(a):
        a = a[tp:].reshape(nseq, ntok, a.shape[-1])
        return jnp.pad(a, ((0, 0), (0, LANE - ntok), (0, 0)))

    o_lat_s = _mla_sample(page_table, q_cat_s, cache_latent, new_rows(lat_b), pg)
    o_lat_s = o_lat_s.reshape(nbs, spb, ntok, MLA_HEADS, KV_RANK)
    o_lat_s = jnp.transpose(o_lat_s, (0, 3, 1, 2, 4)).reshape(nbs, MLA_HEADS, QBLK, KV_RANK)

    q_i_s = hproj[tp:, C_QI:C_QI + IDX_HEADS * IDX_DIM].astype(BF).reshape(nseq, ntok * IDX_HEADS, IDX_DIM)
    w_i_s = hproj[tp:, C_WI:C_WI + IDX_HEADS].reshape(nseq, ntok * IDX_HEADS, 1)
    s_past, s_new = _idx_sample(page_table, q_i_s, w_i_s, cache_idx, new_rows(ki_b), pg, ntok)
    topk_s = min(IDX_TOPK_MAX, (past + ntok) // 4)
    m_past, m_new = _topk_sample(s_past, s_new, topk_s)
    q_b_s = hproj[tp:, C_QB:C_QB + DSA_HEADS * DSA_DIM].astype(BF).reshape(nseq, ntok * DSA_HEADS, DSA_DIM)
    o_b_s = _dsa_sample(page_table, q_b_s, m_past, m_new, cache_kv, new_rows(kv_b), b_last, b_new, pg, ntok)
    o_b_s = o_b_s.reshape(ts, DSA_HEADS * DSA_DIM)

    o_a = jnp.concatenate([_uv(o_lat_p, w_uvt), _uv(o_lat_s, w_uvt)], axis=0)
    o_b = jnp.concatenate([o_b_p, o_b_s], axis=0)
    h1 = _wo_ln(o_a, o_b, w_o_b, x_all, ln1_g, ln1_b, alpha)

    r_idx, r_gate = _router(h1, w_rt_hi, w_rt_lo, b_rt)
    tt = tp + ts
    bm = 3 * MOE_SUB
    nblk = (2 * tt) // bm + N_EXPERTS
    plan = _moe_plan(r_idx[:, 0], r_idx[:, 1], bm, nblk)
    y_sorted = _experts(plan, h1, w_gate, w_up, w_down, bm, nblk)
    out = _combine_ln(plan["pos"], y_sorted, r_gate, h1, ln2_g, ln2_b, alpha)

    kvw = 2 * DSA_DIM
    return (out[:tp].reshape(nbatch, seq, d_model), out[tp:].reshape(nseq, ntok, d_model),
            lat[:tp].reshape(1, nbatch, seq, LATENT_DIM), kvo[:tp].reshape(1, nbatch, seq, kvw),
            kio[:tp].reshape(1, nbatch, seq, IDX_DIM),
            lat[tp:].reshape(1, nseq, ntok, LATENT_DIM), kvo[tp:].reshape(1, nseq, ntok, kvw),
            kio[tp:].reshape(1, nseq, ntok, IDX_DIM))
```

```python
import functools
import math

import jax
import jax.numpy as jnp
from jax import lax
from jax.experimental import pallas as pl
from jax.experimental.pallas import tpu as pltpu

BF = jnp.bfloat16
F32 = jnp.float32
I32 = jnp.int32

MLA_HEADS = 16
MLA_NOPE = 128
MLA_ROPE = 64
MLA_V = 128
Q_RANK = 768
KV_RANK = 512
LATENT_DIM = KV_RANK + MLA_ROPE
ROPE_THETA = 10000.0
MLA_SCALE = (MLA_NOPE + MLA_ROPE) ** -0.5
DSA_HEADS = 16
DSA_DIM = 128
DSA_SCALE = DSA_DIM ** -0.5
IDX_HEADS = 32
IDX_DIM = 128
IDX_SCALE = IDX_DIM ** -0.5
IDX_TOPK_MAX = 256
NUM_BUCKETS = 32
MAX_DISTANCE = 128
N_GROUPS = 8
EXPERTS_PER_GROUP = 8
N_EXPERTS = N_GROUPS * EXPERTS_PER_GROUP
PAGE_SIZE = 128
LN_EPS = 1e-5
RMS_EPS = 1e-6

LANE = 128
QBLK = 128
NEG = -1e30
VMEM_LIMIT = 56 * 1024 * 1024

C_QI = 0
C_QB = C_QI + IDX_HEADS * IDX_DIM
C_CKV = C_QB + DSA_HEADS * DSA_DIM
C_KV = C_CKV + KV_RANK
C_CQ = C_KV + 2 * DSA_DIM
C_KI = C_CQ + Q_RANK
C_KPE = C_KI + IDX_DIM
C_WI = C_KPE + 2 * MLA_ROPE
D_PROJ = C_WI + LANE
assert C_QB % (DSA_HEADS * DSA_DIM) == 0 and C_CKV % KV_RANK == 0 and C_KV % (2 * DSA_DIM) == 0
assert C_CQ % Q_RANK == 0 and C_KI % LANE == 0


def _cparams(n_axes, vmem=VMEM_LIMIT):
    return pltpu.CompilerParams(dimension_semantics=("arbitrary",) * n_axes, vmem_limit_bytes=vmem)


def _pick(n, cands):
    for c in cands:
        if n % c == 0:
            return c
    return n


def _nt_dot(a, b):
    return lax.dot_general(a, b, (((1,), (1,)), ((), ())), preferred_element_type=F32)


def _mm_kernel(x_ref, w_ref, o_ref, xb_ref):
    @pl.when(pl.program_id(1) == 0)
    def _():
        xb_ref[...] = x_ref[...].astype(BF)

    o_ref[...] = _nt_dot(xb_ref[...], w_ref[...]).astype(o_ref.dtype)


def _matmul(x, wt, out_dtype):
    m, k = x.shape
    n = wt.shape[0]
    tm = _pick(m, (512, 256, 128))
    tn = _pick(n, (1152, 1024, 896, 512, 384, 256, 128))
    return pl.pallas_call(
        _mm_kernel,
        grid=(m // tm, n // tn),
        in_specs=[pl.BlockSpec((tm, k), lambda i, j: (i, 0)),
                  pl.BlockSpec((tn, k), lambda i, j: (j, 0))],
        out_specs=pl.BlockSpec((tm, tn), lambda i, j: (i, j)),
        out_shape=jax.ShapeDtypeStruct((m, n), out_dtype),
        scratch_shapes=[pltpu.VMEM((tm, k), BF)],
        compiler_params=_cparams(2),
        name="in_proj",
    )(x, wt)


def _qpath_kernel(cq_ref, g_ref, wn_ref, wp_ref, wuk_ref, cs_ref, o_ref):
    x = cq_ref[...]
    y = x * lax.rsqrt(jnp.mean(x * x, axis=-1, keepdims=True) + RMS_EPS) * g_ref[...]
    yb = y.astype(BF)
    a = jnp.dot(yb, wn_ref[...], preferred_element_type=F32)
    b = jnp.dot(yb, wp_ref[...], preferred_element_type=F32)
    cs = cs_ref[...]
    for h in range(MLA_HEADS):
        qn = a[:, h * LANE:(h + 1) * LANE].astype(BF)
        ql = jnp.dot(qn, wuk_ref[h], preferred_element_type=F32) * MLA_SCALE
        pe = b[:, h * LANE:(h + 1) * LANE] * cs
        pe = (pe + pltpu.roll(pe, MLA_ROPE, 1)) * MLA_SCALE
        o_ref[0, h, :, 0:KV_RANK] = ql.astype(BF)
        o_ref[0, h, :, KV_RANK:LATENT_DIM] = pe[:, 0:MLA_ROPE].astype(BF)


def _qpath(hproj, g_q, w_nope, w_pr, w_ukt, cs):
    tt = hproj.shape[0]
    nb = tt // QBLK
    hw = MLA_HEADS * LANE
    return pl.pallas_call(
        _qpath_kernel,
        grid=(nb,),
        in_specs=[pl.BlockSpec((QBLK, Q_RANK), lambda i: (i, C_CQ // Q_RANK)),
                  pl.BlockSpec((1, Q_RANK), lambda i: (0, 0)),
                  pl.BlockSpec((Q_RANK, hw), lambda i: (0, 0)),
                  pl.BlockSpec((Q_RANK, hw), lambda i: (0, 0)),
                  pl.BlockSpec((MLA_HEADS, MLA_NOPE, KV_RANK), lambda i: (0, 0, 0)),
                  pl.BlockSpec((QBLK, LANE), lambda i: (i, 0))],
        out_specs=pl.BlockSpec((1, MLA_HEADS, QBLK, LATENT_DIM), lambda i: (i, 0, 0, 0)),
        out_shape=jax.ShapeDtypeStruct((nb, MLA_HEADS, QBLK, LATENT_DIM), BF),
        compiler_params=_cparams(1),
        name="mla_qpath",
    )(hproj, g_q, w_nope, w_pr, w_ukt, cs)


def _kvpath_kernel(ckv_ref, kpe_ref, kv_ref, ki_ref, g_ref, cs_ref,
                   lat_ref, kvo_ref, kio_ref, latb_ref, kvb_ref, kib_ref):
    x = ckv_ref[...]
    c = x * lax.rsqrt(jnp.mean(x * x, axis=-1, keepdims=True) + RMS_EPS) * g_ref[...]
    pe = kpe_ref[...] * cs_ref[...]
    pe = (pe + pltpu.roll(pe, MLA_ROPE, 1))[:, 0:MLA_ROPE]
    lat_ref[:, 0:KV_RANK] = c
    lat_ref[:, KV_RANK:LATENT_DIM] = pe
    latb_ref[:, 0:KV_RANK] = c.astype(BF)
    latb_ref[:, KV_RANK:LATENT_DIM] = pe.astype(BF)
    kv = kv_ref[...]
    kvo_ref[...] = kv
    kvb_ref[...] = kv.astype(BF)
    ki = ki_ref[...]
    kio_ref[...] = ki
    kib_ref[...] = ki.astype(BF)


def _kvpath(hproj, g_kv, cs):
    tt = hproj.shape[0]
    tm = _pick(tt, (256, 128))
    kvw = 2 * DSA_DIM
    outs = [jax.ShapeDtypeStruct((tt, LATENT_DIM), F32), jax.ShapeDtypeStruct((tt, kvw), F32),
            jax.ShapeDtypeStruct((tt, IDX_DIM), F32), jax.ShapeDtypeStruct((tt, LATENT_DIM), BF),
            jax.ShapeDtypeStruct((tt, kvw), BF), jax.ShapeDtypeStruct((tt, IDX_DIM), BF)]
    ospec = [pl.BlockSpec((tm, LATENT_DIM), lambda i: (i, 0)), pl.BlockSpec((tm, kvw), lambda i: (i, 0)),
             pl.BlockSpec((tm, IDX_DIM), lambda i: (i, 0))]
    return pl.pallas_call(
        _kvpath_kernel,
        grid=(tt // tm,),
        in_specs=[pl.BlockSpec((tm, KV_RANK), lambda i: (i, C_CKV // KV_RANK)),
                  pl.BlockSpec((tm, LANE), lambda i: (i, C_KPE // LANE)),
                  pl.BlockSpec((tm, kvw), lambda i: (i, C_KV // kvw)),
                  pl.BlockSpec((tm, IDX_DIM), lambda i: (i, C_KI // IDX_DIM)),
                  pl.BlockSpec((1, KV_RANK), lambda i: (0, 0)),
                  pl.BlockSpec((tm, LANE), lambda i: (i, 0))],
        out_specs=ospec + ospec,
        out_shape=outs,
        compiler_params=_cparams(1),
        name="kv_path",
    )(hproj, hproj, hproj, hproj, g_kv, cs)


def _mla_prompt_kernel(q_ref, lat_ref, o_ref, m_ref, l_ref, acc_ref, *, kc):
    i = pl.program_id(1)
    rows = MLA_HEADS * QBLK
    q = q_ref[0].reshape(rows, LATENT_DIM)
    m_ref[...] = jnp.full(m_ref.shape, NEG, F32)
    l_ref[...] = jnp.zeros(l_ref.shape, F32)
    acc_ref[...] = jnp.zeros(acc_ref.shape, F32)
    row_t = i * QBLK + (lax.broadcasted_iota(I32, (rows, kc), 0) & (QBLK - 1))
    col = lax.broadcasted_iota(I32, (rows, kc), 1)
    n_chunks = ((i + 1) * QBLK + kc - 1) // kc

    def body(c, carry):
        off = pl.multiple_of(c * kc, kc)
        k = lat_ref[0, pl.ds(off, kc), :]
        s = _nt_dot(q, k)
        s = jnp.where(col + off <= row_t, s, NEG)
        m_old = m_ref[...]
        m_new = jnp.maximum(m_old, jnp.max(s, axis=-1, keepdims=True))
        a = jnp.exp(m_old - m_new)
        p = jnp.exp(s - m_new)
        l_ref[...] = a * l_ref[...] + jnp.sum(p, axis=-1, keepdims=True)
        acc_ref[...] = a * acc_ref[...] + jnp.dot(p.astype(BF), k[:, 0:KV_RANK], preferred_element_type=F32)
        m_ref[...] = m_new
        return carry

    lax.fori_loop(0, n_chunks, body, 0)
    o = acc_ref[...] / l_ref[...]
    o_ref[0] = o.reshape(MLA_HEADS, QBLK, KV_RANK).astype(o_ref.dtype)


def _mla_prompt(q_cat, lat_b):
    nbatch, seq, _ = lat_b.shape
    nbq = seq // QBLK
    kc = _pick(seq, (256, 128))
    rows = MLA_HEADS * QBLK
    return pl.pallas_call(
        functools.partial(_mla_prompt_kernel, kc=kc),
        grid=(nbatch, nbq),
        in_specs=[pl.BlockSpec((1, MLA_HEADS, QBLK, LATENT_DIM), lambda b, i: (b * nbq + i, 0, 0, 0)),
                  pl.BlockSpec((1, seq, LATENT_DIM), lambda b, i: (b, 0, 0))],
        out_specs=pl.BlockSpec((1, MLA_HEADS, QBLK, KV_RANK), lambda b, i: (b * nbq + i, 0, 0, 0)),
        out_shape=jax.ShapeDtypeStruct((nbatch * nbq, MLA_HEADS, QBLK, KV_RANK), BF),
        scratch_shapes=[pltpu.VMEM((rows, 1), F32), pltpu.VMEM((rows, 1), F32), pltpu.VMEM((rows, KV_RANK), F32)],
        compiler_params=_cparams(2),
        name="mla_prompt",
    )(q_cat, lat_b)


def _uv_kernel(o_ref, w_ref, out_ref):
    for h in range(MLA_HEADS):
        out_ref[:, h * MLA_V:(h + 1) * MLA_V] = jnp.dot(
            o_ref[0, h], w_ref[h], preferred_element_type=F32).astype(out_ref.dtype)


def _uv(o_lat, w_uvt):
    nb = o_lat.shape[0]
    return pl.pallas_call(
        _uv_kernel,
        grid=(nb,),
        in_specs=[pl.BlockSpec((1, MLA_HEADS, QBLK, KV_RANK), lambda i: (i, 0, 0, 0)),
                  pl.BlockSpec((MLA_HEADS, KV_RANK, MLA_V), lambda i: (0, 0, 0))],
        out_specs=pl.BlockSpec((QBLK, MLA_HEADS * MLA_V), lambda i: (i, 0)),
        out_shape=jax.ShapeDtypeStruct((nb * QBLK, MLA_HEADS * MLA_V), BF),
        compiler_params=_cparams(1),
        name="mla_uv",
    )(o_lat, w_uvt)


def _float_key(x):
    bits = pltpu.bitcast(x, I32)
    return jnp.where(bits < 0, bits ^ jnp.int32(0x7FFFFFFF), bits)


def _topk_threshold(key_ref, k):
    nrows, ncols = key_ref.shape
    idx_bits = int(math.ceil(math.log2(ncols))) + 1
    kf = jnp.float32(k)

    def count(pred):
        return jnp.sum(jnp.where(pred, 1.0, 0.0), axis=-1, keepdims=True)

    def vbody(b, thr):
        cand = thr + lax.shift_left(jnp.int32(1), 31 - b)
        return jnp.where(count(key_ref[...] >= cand) >= kf, cand, thr)

    thr = lax.fori_loop(0, 32, vbody, jnp.full((nrows, 1), -2 ** 31, I32))
    key = key_ref[...]
    need = kf - count(key > thr)
    some_row_cuts = jnp.max(jnp.where(count(key == thr) == need, 0.0, 1.0)) > 0.0

    def cbody(b, cut):
        cand = cut + lax.shift_left(jnp.int32(1), idx_bits - 1 - b)
        col = lax.broadcasted_iota(I32, (nrows, ncols), 1)
        n = jnp.sum(jnp.where(key_ref[...] == thr, jnp.where(col < cand, 1.0, 0.0), 0.0), axis=-1, keepdims=True)
        return jnp.where(n <= need, cand, cut)

    cut0 = jnp.where(some_row_cuts, jnp.zeros((nrows, 1), I32), jnp.full((nrows, 1), 2 ** idx_bits, I32))
    cut = lax.fori_loop(0, jnp.where(some_row_cuts, idx_bits, 0), cbody, cut0)
    return thr, cut


def _selected(key, col, thr, cut):
    return (key > thr) | ((key == thr) & (col < cut))


def _dsa_prompt_kernel(qi_ref, wi_ref, qb_ref, ki_ref, kv_ref, bt_ref, o_ref,
                       qa_ref, wb_ref, sc_ref, key_ref, lg_ref, *, topk, kch):
    i = pl.program_id(1)
    seq = ki_ref.shape[1]
    w = wi_ref[...] * (IDX_SCALE * IDX_HEADS ** -0.5)
    lane = lax.broadcasted_iota(I32, (QBLK, LANE), 1)
    for hh in range(IDX_HEADS):
        qa_ref[hh * QBLK:(hh + 1) * QBLK, :] = qi_ref[:, hh * IDX_DIM:(hh + 1) * IDX_DIM].astype(BF)
        wcol = jnp.sum(jnp.where(lane == hh, w, 0.0), axis=-1, keepdims=True)
        wb_ref[hh] = jnp.broadcast_to(wcol, (QBLK, LANE))
    off = pl.multiple_of(i * QBLK, QBLK)

    def body(kend):
        for ch in range(kend // kch):
            d = _nt_dot(qa_ref[...], ki_ref[0, ch * kch:(ch + 1) * kch, :])
            acc = None
            for hh in range(IDX_HEADS):
                wt = jnp.concatenate([wb_ref[hh]] * (kch // LANE), axis=1)
                term = jnp.maximum(d[hh * QBLK:(hh + 1) * QBLK], 0.0) * wt
                acc = term if acc is None else acc + term
            sc_ref[:, ch * kch:(ch + 1) * kch] = acc

        t_idx = i * QBLK + lax.broadcasted_iota(I32, (QBLK, kend), 0)
        s_idx = lax.broadcasted_iota(I32, (QBLK, kend), 1)
        causal = s_idx <= t_idx
        key_ref[:, 0:kend] = _float_key(jnp.where(causal, sc_ref[:, 0:kend], -jnp.inf))
        thr, cut = _topk_threshold(key_ref.at[:, 0:kend], topk)
        sc_ref[:, 0:kend] = jnp.where(_selected(key_ref[:, 0:kend], s_idx, thr, cut) & causal, 0.0, NEG)

        def abody(h, carry):
            hoff = pl.multiple_of(h * DSA_DIM, DSA_DIM)
            q = qb_ref[:, pl.ds(hoff, DSA_DIM)].astype(BF)
            lg_ref[:, 0:kend] = _nt_dot(q, kv_ref[0, 0:kend, 0:DSA_DIM]) * DSA_SCALE + sc_ref[:, 0:kend]
            lg_ref[:, pl.ds(off, QBLK)] += bt_ref[h, :, QBLK:2 * QBLK]

            @pl.when(i >= 1)
            def _():
                lg_ref[:, pl.ds(off - QBLK, QBLK)] += bt_ref[h, :, 0:QBLK]

            lg = lg_ref[:, 0:kend]
            m = jnp.max(lg, axis=-1, keepdims=True)
            p = jnp.exp(lg - m)
            l = jnp.sum(p, axis=-1, keepdims=True)
            o = jnp.dot(p.astype(BF), kv_ref[0, 0:kend, DSA_DIM:2 * DSA_DIM], preferred_element_type=F32) / l
            o_ref[:, pl.ds(hoff, DSA_DIM)] = o.astype(o_ref.dtype)
            return carry

        lax.fori_loop(0, DSA_HEADS, abody, 0)

    n_ch = (off + QBLK + kch - 1) // kch
    for n in range(1, seq // kch + 1):
        pl.when(n_ch == n)(functools.partial(body, n * kch))


def _dsa_prompt(hproj, ki_b, kv_b, bt, nbatch, seq, topk):
    nbq = seq // QBLK
    qiw = IDX_HEADS * IDX_DIM
    qbw = DSA_HEADS * DSA_DIM
    kch = _pick(seq, (512, 256, 128))
    assert kch >= topk
    return pl.pallas_call(
        functools.partial(_dsa_prompt_kernel, topk=topk, kch=kch),
        grid=(nbatch, nbq),
        in_specs=[pl.BlockSpec((QBLK, qiw), lambda b, i: (b * nbq + i, C_QI // qiw)),
                  pl.BlockSpec((QBLK, LANE), lambda b, i: (b * nbq + i, C_WI // LANE)),
                  pl.BlockSpec((QBLK, qbw), lambda b, i: (b * nbq + i, C_QB // qbw)),
                  pl.BlockSpec((1, seq, IDX_DIM), lambda b, i: (b, 0, 0)),
                  pl.BlockSpec((1, seq, 2 * DSA_DIM), lambda b, i: (b, 0, 0)),
                  pl.BlockSpec((DSA_HEADS, QBLK, 2 * QBLK), lambda b, i: (0, 0, 0))],
        out_specs=pl.BlockSpec((QBLK, qbw), lambda b, i: (b * nbq + i, 0)),
        out_shape=jax.ShapeDtypeStruct((nbatch * seq, qbw), BF),
        scratch_shapes=[pltpu.VMEM((IDX_HEADS * QBLK, IDX_DIM), BF), pltpu.VMEM((IDX_HEADS, QBLK, LANE), F32),
                        pltpu.VMEM((QBLK, seq), F32), pltpu.VMEM((QBLK, seq), I32), pltpu.VMEM((QBLK, seq), F32)],
        compiler_params=_cparams(2),
        name="dsa_prompt",
    )(hproj, hproj, hproj, ki_b, kv_b, bt)


PAGE_GROUP = 8


def _page_pipeline(pt_ref, cache_hbm, buf_ref, sem_ref, pg):
    nc = pl.num_programs(1)
    total = pl.num_programs(0) * nc
    t = pl.program_id(0) * nc + pl.program_id(1)
    slot = lax.rem(t, 2)

    def page_copy(page, dst_slot, j):
        return pltpu.make_async_copy(cache_hbm.at[0, page], buf_ref.at[dst_slot, j], sem_ref.at[dst_slot])

    def start(step, dst_slot):
        s = step // nc
        c = lax.rem(step, nc)
        for j in range(pg):
            page_copy(pt_ref[s, c * pg + j], dst_slot, j).start()

    @pl.when(t == 0)
    def _():
        start(t, slot)

    @pl.when(t + 1 < total)
    def _():
        start(t + 1, 1 - slot)

    for j in range(pg):
        page_copy(0, slot, j).wait()
    return slot


def _paged_call(kernel_fn, name, page_table, cache, pg, page_shape, in_arrays, in_specs, out_shape, out_specs,
                scratch_shapes):
    nseq, npages = page_table.shape
    grid_spec = pltpu.PrefetchScalarGridSpec(
        num_scalar_prefetch=1,
        grid=(nseq, npages // pg),
        in_specs=in_specs + [pl.BlockSpec(memory_space=pl.ANY)],
        out_specs=out_specs,
        scratch_shapes=[pltpu.VMEM((2, pg) + page_shape, cache.dtype), pltpu.SemaphoreType.DMA((2,))]
        + scratch_shapes,
    )
    return pl.pallas_call(kernel_fn, grid_spec=grid_spec, out_shape=out_shape, compiler_params=_cparams(2),
                          name=name)(page_table, *in_arrays, cache)


def _group_scores(d, wcol, ntok):
    x = jnp.maximum(d, 0.0) * wcol
    return jnp.concatenate(
        [jnp.sum(x[t * IDX_HEADS:(t + 1) * IDX_HEADS], axis=0, keepdims=True) for t in range(ntok)], axis=0)


def _idx_sample_kernel(pt_ref, q_ref, w_ref, knew_ref, cache_hbm, o_ref, onew_ref, buf_ref, sem_ref, *, pg, ntok):
    slot = _page_pipeline(pt_ref, cache_hbm, buf_ref, sem_ref, pg)
    c = pl.program_id(1)
    q = q_ref[0]
    wcol = w_ref[0] * (IDX_SCALE * IDX_HEADS ** -0.5)
    grp = min(PAGE_GROUP, pg)
    for g in range(pg // grp):
        kc = jnp.concatenate([buf_ref[slot, g * grp + j].astype(BF) for j in range(grp)], axis=0)
        o_ref[0, :, g * grp * PAGE_SIZE:(g + 1) * grp * PAGE_SIZE] = _group_scores(_nt_dot(q, kc), wcol, ntok)

    @pl.when(c == pl.num_programs(1) - 1)
    def _():
        sn = _group_scores(_nt_dot(q, knew_ref[0]), wcol, ntok)
        row = lax.broadcasted_iota(I32, (ntok, LANE), 0)
        colv = lax.broadcasted_iota(I32, (ntok, LANE), 1)
        onew_ref[0] = jnp.where(colv <= row, sn, -jnp.inf)


def _idx_sample(page_table, q_idx, w_col, cache_idx, k_new, pg, ntok):
    nseq, npages = page_table.shape
    sc = pg * PAGE_SIZE
    rows = ntok * IDX_HEADS
    return _paged_call(
        functools.partial(_idx_sample_kernel, pg=pg, ntok=ntok), "idx_sample", page_table, cache_idx, pg,
        (PAGE_SIZE, IDX_DIM), [q_idx, w_col, k_new],
        [pl.BlockSpec((1, rows, IDX_DIM), lambda s, c, pt: (s, 0, 0)),
         pl.BlockSpec((1, rows, 1), lambda s, c, pt: (s, 0, 0)),
         pl.BlockSpec((1, LANE, IDX_DIM), lambda s, c, pt: (s, 0, 0))],
        [jax.ShapeDtypeStruct((nseq, ntok, npages * PAGE_SIZE), F32), jax.ShapeDtypeStruct((nseq, ntok, LANE), F32)],
        [pl.BlockSpec((1, ntok, sc), lambda s, c, pt: (s, 0, c)),
         pl.BlockSpec((1, ntok, LANE), lambda s, c, pt: (s, 0, 0))],
        [])


def _topk_sample_kernel(sp_ref, sn_ref, thr_ref, cut_ref, key_ref, *, topk):
    past = sp_ref.shape[1]
    key_ref[:, 0:past] = _float_key(sp_ref[...])
    key_ref[:, past:past + LANE] = _float_key(sn_ref[...])
    thr, cut = _topk_threshold(key_ref, topk)
    thr_ref[...] = jnp.broadcast_to(thr, thr_ref.shape)
    cut_ref[...] = jnp.broadcast_to(cut, cut_ref.shape)


def _topk_sample(s_past, s_new, topk):
    nrows, past = s_past.shape
    tr = _pick(nrows, (64, 32, 16, 8))
    return pl.pallas_call(
        functools.partial(_topk_sample_kernel, topk=topk),
        grid=(nrows // tr,),
        in_specs=[pl.BlockSpec((tr, past), lambda r: (r, 0)),
                  pl.BlockSpec((tr, LANE), lambda r: (r, 0))],
        out_specs=[pl.BlockSpec((tr, LANE), lambda r: (r, 0)),
                   pl.BlockSpec((tr, LANE), lambda r: (r, 0))],
        out_shape=[jax.ShapeDtypeStruct((nrows, LANE), I32), jax.ShapeDtypeStruct((nrows, LANE), I32)],
        scratch_shapes=[pltpu.VMEM((tr, past + LANE), I32)],
        compiler_params=_cparams(1),
        name="topk_sample",
    )(s_past, s_new)


def _flash_init(m_ref, l_ref, acc_ref):
    m_ref[...] = jnp.full(m_ref.shape, NEG, F32)
    l_ref[...] = jnp.zeros(l_ref.shape, F32)
    acc_ref[...] = jnp.zeros(acc_ref.shape, F32)


def _flash_step(s, v, m_ref, l_ref, acc_ref, v_transposed=False):
    m_old = m_ref[...]
    m_new = jnp.maximum(m_old, jnp.max(s, axis=-1, keepdims=True))
    a = jnp.exp(m_old - m_new)
    p = jnp.exp(s - m_new)
    l_ref[...] = a * l_ref[...] + jnp.sum(p, axis=-1, keepdims=True)
    pb = p.astype(BF)
    pv = _nt_dot(pb, v) if v_transposed else jnp.dot(pb, v, preferred_element_type=F32)
    acc_ref[...] = a * acc_ref[...] + pv
    m_ref[...] = m_new


def _mla_sample_kernel(pt_ref, q_ref, new_ref, cache_hbm, o_ref, buf_ref, sem_ref, m_ref, l_ref, acc_ref, kt_ref,
                       *, pg, nheads):
    slot = _page_pipeline(pt_ref, cache_hbm, buf_ref, sem_ref, pg)
    c = pl.program_id(1)

    @pl.when(c == 0)
    def _():
        _flash_init(m_ref, l_ref, acc_ref)

    q = q_ref[0]
    for j in range(pg):
        kt_ref[:, j * PAGE_SIZE:(j + 1) * PAGE_SIZE] = buf_ref[slot, j].astype(BF)
    s = jnp.dot(q, kt_ref[...], preferred_element_type=F32)
    _flash_step(s, kt_ref[0:KV_RANK, :], m_ref, l_ref, acc_ref, v_transposed=True)

    @pl.when(c == pl.num_programs(1) - 1)
    def _():
        kn = new_ref[0]
        rows = q.shape[0]
        s = _nt_dot(q, kn)
        t = lax.broadcasted_iota(I32, (rows, LANE), 0) // nheads
        colv = lax.broadcasted_iota(I32, (rows, LANE), 1)
        _flash_step(jnp.where(colv <= t, s, NEG), kn[:, 0:KV_RANK], m_ref, l_ref, acc_ref)
        o_ref[0] = (acc_ref[...] / l_ref[...]).astype(o_ref.dtype)


def _mla_sample(page_table, q_cat, cache_latent_t, lat_new, pg):
    nseq = page_table.shape[0]
    rows = q_cat.shape[1]
    return _paged_call(
        functools.partial(_mla_sample_kernel, pg=pg, nheads=MLA_HEADS), "mla_sample", page_table, cache_latent_t, pg,
        (LATENT_DIM, PAGE_SIZE), [q_cat, lat_new],
        [pl.BlockSpec((1, rows, LATENT_DIM), lambda s, c, pt: (s, 0, 0)),
         pl.BlockSpec((1, LANE, LATENT_DIM), lambda s, c, pt: (s, 0, 0))],
        jax.ShapeDtypeStruct((nseq, rows, KV_RANK), BF),
        pl.BlockSpec((1, rows, KV_RANK), lambda s, c, pt: (s, 0, 0)),
        [pltpu.VMEM((rows, 1), F32), pltpu.VMEM((rows, 1), F32), pltpu.VMEM((rows, KV_RANK), F32),
         pltpu.VMEM((LATENT_DIM, pg * PAGE_SIZE), BF)])


def _expand_rows(mask, ntok, nheads):
    return jnp.concatenate(
        [jnp.broadcast_to(mask[t:t + 1], (nheads, mask.shape[1])) for t in range(ntok)], axis=0)


def _dsa_sample_kernel(pt_ref, q_ref, sp_ref, sn_ref, thr_ref, cut_ref, new_ref, blast_ref, bnew_ref, cache_hbm,
                       o_ref, buf_ref, sem_ref, m_ref, l_ref, acc_ref, kvb_ref, *, pg, ntok):
    slot = _page_pipeline(pt_ref, cache_hbm, buf_ref, sem_ref, pg)
    c = pl.program_id(1)
    last = pl.num_programs(1) - 1
    sc = pg * PAGE_SIZE

    @pl.when(c == 0)
    def _():
        _flash_init(m_ref, l_ref, acc_ref)

    q = q_ref[0]
    thr = thr_ref[0][:, 0:1]
    cut = cut_ref[0][:, 0:1]

    def mask_rows(scores, col):
        sel = _selected(_float_key(scores), col, thr, cut)
        return _expand_rows(jnp.where(sel, 0.0, NEG), ntok, DSA_HEADS)

    near_bias = jnp.where(c == last, blast_ref[...], 0.0)
    for j in range(pg):
        kvb_ref[j * PAGE_SIZE:(j + 1) * PAGE_SIZE, :] = buf_ref[slot, j].astype(BF)
    col = c * sc + lax.broadcasted_iota(I32, (ntok, sc), 1)
    lg = _nt_dot(q, kvb_ref[:, 0:DSA_DIM]) * DSA_SCALE + mask_rows(sp_ref[0], col)
    tail = lg[:, sc - LANE:sc] + near_bias
    lg = tail if sc == LANE else jnp.concatenate([lg[:, 0:sc - LANE], tail], axis=1)
    _flash_step(lg, kvb_ref[:, DSA_DIM:2 * DSA_DIM], m_ref, l_ref, acc_ref)

    @pl.when(c == last)
    def _():
        kvn = new_ref[0]
        past = pl.num_programs(1) * sc
        coln = lax.broadcasted_iota(I32, (ntok, LANE), 1)
        valid = coln <= lax.broadcasted_iota(I32, (ntok, LANE), 0)
        madd = jnp.where(valid & _selected(_float_key(sn_ref[0]), past + coln, thr, cut), 0.0, NEG)
        ln = (_nt_dot(q, kvn[:, 0:DSA_DIM]) * DSA_SCALE + bnew_ref[...] + _expand_rows(madd, ntok, DSA_HEADS))
        _flash_step(ln, kvn[:, DSA_DIM:2 * DSA_DIM], m_ref, l_ref, acc_ref)
        o_ref[0] = (acc_ref[...] / l_ref[...]).astype(o_ref.dtype)


def _dsa_sample(page_table, q_b, s_past, s_new, thr, cut, cache_kv, kv_new, b_last, b_new, pg, ntok):
    nseq = page_table.shape[0]
    rows = q_b.shape[1]
    sc = pg * PAGE_SIZE
    per_seq = lambda s, c, pt: (s, 0, 0)
    return _paged_call(
        functools.partial(_dsa_sample_kernel, pg=pg, ntok=ntok), "dsa_sample", page_table, cache_kv, pg,
        (PAGE_SIZE, 2 * DSA_DIM), [q_b, s_past, s_new, thr, cut, kv_new, b_last, b_new],
        [pl.BlockSpec((1, rows, DSA_DIM), per_seq),
         pl.BlockSpec((1, ntok, sc), lambda s, c, pt: (s, 0, c)),
         pl.BlockSpec((1, ntok, LANE), per_seq),
         pl.BlockSpec((1, ntok, LANE), per_seq),
         pl.BlockSpec((1, ntok, LANE), per_seq),
         pl.BlockSpec((1, LANE, 2 * DSA_DIM), per_seq),
         pl.BlockSpec((rows, LANE), lambda s, c, pt: (0, 0)),
         pl.BlockSpec((rows, LANE), lambda s, c, pt: (0, 0))],
        jax.ShapeDtypeStruct((nseq, rows, DSA_DIM), BF),
        pl.BlockSpec((1, rows, DSA_DIM), per_seq),
        [pltpu.VMEM((rows, 1), F32), pltpu.VMEM((rows, 1), F32), pltpu.VMEM((rows, DSA_DIM), F32),
         pltpu.VMEM((sc, 2 * DSA_DIM), BF)])


def _layernorm(z, g, b):
    mu = jnp.mean(z, axis=-1, keepdims=True)
    zc = z - mu
    var = jnp.mean(zc * zc, axis=-1, keepdims=True)
    return zc * lax.rsqrt(var + LN_EPS) * g + b


def _wo_ln_kernel(oap_ref, obp_ref, oas_ref, obs_ref, w_ref, xp_hbm, xs_hbm, g_ref, b_ref, o_ref, x_ref, sem,
                  *, npt, nka, alpha):
    i = pl.program_id(0)
    k = pl.program_id(1)
    tm = o_ref.shape[0]

    def x_copy(is_sample):
        src, row0 = (xs_hbm, (i - npt) * tm) if is_sample else (xp_hbm, i * tm)
        return pltpu.make_async_copy(src.at[pl.ds(pl.multiple_of(row0, tm), tm)], x_ref, sem)

    for is_sample in (False, True):
        @pl.when((k == 0) & ((i >= npt) == is_sample))
        def _():
            x_copy(is_sample).start()

    @pl.when(k == 0)
    def _():
        o_ref[...] = jnp.zeros(o_ref.shape, F32)

    for lhs_ref, is_sample, is_b in ((oap_ref, False, False), (obp_ref, False, True),
                                      (oas_ref, True, False), (obs_ref, True, True)):
        @pl.when(((i >= npt) == is_sample) & ((k >= nka) == is_b))
        def _():
            o_ref[...] += jnp.dot(lhs_ref[...], w_ref[...], preferred_element_type=F32)

    for is_sample in (False, True):
        @pl.when((k == pl.num_programs(1) - 1) & ((i >= npt) == is_sample))
        def _():
            x_copy(is_sample).wait()
            o_ref[...] = _layernorm(alpha * x_ref[...] + o_ref[...], g_ref[...], b_ref[...])


def _wo_ln(oa_p, ob_p, oa_s, ob_s, w_o, x_p, x_s, g, b, alpha):
    tp, d = x_p.shape
    ts = x_s.shape[0]
    wa = oa_p.shape[1]
    tm = _pick(math.gcd(tp, ts), (512, 256, 128))
    tk = _pick(wa, (512, 256, 128))
    npt, nst = tp // tm, ts // tm
    nka = wa // tk
    nkb = ob_p.shape[1] // tk

    def lhs_spec(is_sample, is_b):
        def index(i, k):
            row = jnp.clip(i - npt, 0, nst - 1) if is_sample else jnp.minimum(i, npt - 1)
            col = jnp.clip(k - nka, 0, nkb - 1) if is_b else jnp.minimum(k, nka - 1)
            return (row, jnp.where((i >= npt) == is_sample, col, 0))
        return pl.BlockSpec((tm, tk), index)

    return pl.pallas_call(
        functools.partial(_wo_ln_kernel, npt=npt, nka=nka, alpha=alpha),
        grid=(npt + nst, nka + nkb),
        in_specs=[lhs_spec(False, False), lhs_spec(False, True), lhs_spec(True, False), lhs_spec(True, True),
                  pl.BlockSpec((tk, d), lambda i, k: (k, 0)),
                  pl.BlockSpec(memory_space=pl.ANY),
                  pl.BlockSpec(memory_space=pl.ANY),
                  pl.BlockSpec((1, d), lambda i, k: (0, 0)),
                  pl.BlockSpec((1, d), lambda i, k: (0, 0))],
        out_specs=pl.BlockSpec((tm, d), lambda i, k: (i, 0)),
        out_shape=jax.ShapeDtypeStruct((tp + ts, d), F32),
        scratch_shapes=[pltpu.VMEM((tm, d), F32), pltpu.SemaphoreType.DMA(())],
        compiler_params=_cparams(2),
        name="wo_ln1",
    )(oa_p, ob_p, oa_s, ob_s, w_o, x_p, x_s, g, b)


def _router_kernel(h_ref, whi_ref, wlo_ref, b_ref, oi_ref, of_ref):
    h = h_ref[...]
    hhi = h.astype(BF)
    hlo = (h - hhi.astype(F32)).astype(BF)
    whi = whi_ref[...]
    logits = (jnp.dot(hhi, whi, preferred_element_type=F32)
              + jnp.dot(hhi, wlo_ref[...], preferred_element_type=F32)
              + jnp.dot(hlo, whi, preferred_element_type=F32)) + b_ref[...]
    lane = lax.broadcasted_iota(I32, logits.shape, 1)
    big = jnp.int32(2 ** 30)
    is_g = lane < N_GROUPS
    lgp = jnp.where(is_g, logits, -jnp.inf)
    mg = jnp.max(lgp, axis=-1, keepdims=True)
    p_grp = 1.0 / jnp.sum(jnp.exp(lgp - mg), axis=-1, keepdims=True)
    grp = jnp.min(jnp.where(lgp == mg, lane, big), axis=-1, keepdims=True)
    lo = N_GROUPS + grp * EXPERTS_PER_GROUP
    in_g = (lane >= lo) & (lane < lo + EXPERTS_PER_GROUP)
    le = jnp.where(in_g, logits, -jnp.inf)
    m1 = jnp.max(le, axis=-1, keepdims=True)
    i1 = jnp.min(jnp.where(le == m1, lane, big), axis=-1, keepdims=True)
    le2 = jnp.where(lane == i1, -jnp.inf, le)
    m2 = jnp.max(le2, axis=-1, keepdims=True)
    i2 = jnp.min(jnp.where(le2 == m2, lane, big), axis=-1, keepdims=True)
    e2 = jnp.exp(m2 - m1)
    g1 = p_grp / (1.0 + e2)
    g2 = p_grp * e2 / (1.0 + e2)
    oi_ref[...] = jnp.where(lane == 0, i1 - N_GROUPS, jnp.where(lane == 1, i2 - N_GROUPS, 0))
    of_ref[...] = jnp.where(lane == 0, g1, jnp.where(lane == 1, g2, 0.0))


def _router(h, w_hi, w_lo, bias):
    tt, d = h.shape
    tm = _pick(tt, (512, 256, 128))
    return pl.pallas_call(
        _router_kernel,
        grid=(tt // tm,),
        in_specs=[pl.BlockSpec((tm, d), lambda i: (i, 0)),
                  pl.BlockSpec((d, LANE), lambda i: (0, 0)),
                  pl.BlockSpec((d, LANE), lambda i: (0, 0)),
                  pl.BlockSpec((1, LANE), lambda i: (0, 0))],
        out_specs=[pl.BlockSpec((tm, LANE), lambda i: (i, 0)), pl.BlockSpec((tm, LANE), lambda i: (i, 0))],
        out_shape=[jax.ShapeDtypeStruct((tt, LANE), I32), jax.ShapeDtypeStruct((tt, LANE), F32)],
        compiler_params=_cparams(1),
        name="moe_router",
    )(h, w_hi, w_lo, bias)


MOE_BM = 384


MOE_DMA_SPLIT = 8


def _experts_kernel(bexp_ref, brows_ref, bsrc_ref, tok_ref, h_hbm, wg_hbm, wu_hbm, wd_hbm, o_ref,
                    xf_ref, xb_ref, g_ref, u_ref, hb_ref, wgu_ref, wdn_ref, sem_x, sem_gu, sem_dn,
                    *, bm, nka, nkb, tka, tn):
    j = pl.program_id(0)
    k = pl.program_id(1)
    nblk = pl.num_programs(0)
    nsteps = nka + nkb
    na = tok_ref.shape[0]
    de = wdn_ref.shape[1]

    def gather(blk, start):
        def body(r, carry):
            tok = tok_ref[jnp.minimum(bsrc_ref[blk] + r, na - 1)] if start else 0
            cp = pltpu.make_async_copy(h_hbm.at[pl.ds(tok, 1)], xf_ref.at[pl.ds(r, 1)], sem_x)
            if start:
                cp.start()
            else:
                cp.wait()
            return carry

        lax.fori_loop(0, bm, body, 0, unroll=8)

    def w_tile(blk, step, start):
        step = jnp.asarray(step, I32)
        e = bexp_ref[blk]

        def run(cp):
            if start:
                cp.start()
            else:
                cp.wait()

        @pl.when(step < nka)
        def _():
            slot = lax.rem(step, 2)
            rq = tka // MOE_DMA_SPLIT
            for q in range(MOE_DMA_SPLIT):
                rows = pl.ds(pl.multiple_of(step * tka + q * rq, rq), rq)
                for which, w_hbm in enumerate((wg_hbm, wu_hbm)):
                    run(pltpu.make_async_copy(w_hbm.at[0, e, rows, :],
                                              wgu_ref.at[slot, which, pl.ds(q * rq, rq), :], sem_gu.at[slot]))

        @pl.when(step >= nka)
        def _():
            f = step - nka
            slot = lax.rem(f, 2)
            rq = de // MOE_DMA_SPLIT
            cols = pl.ds(pl.multiple_of(f * tn, tn), tn)
            for q in range(MOE_DMA_SPLIT):
                run(pltpu.make_async_copy(wd_hbm.at[0, e, pl.ds(q * rq, rq), cols],
                                          wdn_ref.at[slot, pl.ds(q * rq, rq), :], sem_dn.at[slot]))

    @pl.when((brows_ref[j] == 0) & (k == 0))
    def _():
        o_ref[...] = jnp.zeros(o_ref.shape, F32)

    @pl.when(brows_ref[j] > 0)
    def _():
        nxt = jnp.minimum(j + 1, nblk - 1)
        has_next = (j + 1 < nblk) & (brows_ref[nxt] > 0)

        @pl.when((j == 0) & (k == 0))
        def _():
            w_tile(j, k, True)
            gather(j, True)

        @pl.when(k + 1 < nsteps)
        def _():
            w_tile(j, k + 1, True)

        @pl.when((k + 1 == nsteps) & has_next)
        def _():
            w_tile(nxt, 0, True)

        @pl.when(k == 0)
        def _():
            gather(j, False)
            xb_ref[...] = xf_ref[...].astype(BF)

            @pl.when(has_next)
            def _():
                gather(nxt, True)

            g_ref[...] = jnp.zeros(g_ref.shape, F32)
            u_ref[...] = jnp.zeros(u_ref.shape, F32)

        w_tile(j, k, False)

        @pl.when(k < nka)
        def _():
            slot = lax.rem(k, 2)
            xk = xb_ref[:, pl.ds(pl.multiple_of(k * tka, tka), tka)]
            g_ref[...] += jnp.dot(xk, wgu_ref[slot, 0].astype(BF), preferred_element_type=F32)
            u_ref[...] += jnp.dot(xk, wgu_ref[slot, 1].astype(BF), preferred_element_type=F32)

        @pl.when(k == nka)
        def _():
            g = g_ref[...]
            hb_ref[...] = (g * jax.nn.sigmoid(g) * u_ref[...]).astype(BF)

        for f in range(nkb):
            @pl.when(k == nka + f)
            def _():
                o_ref[:, f * tn:(f + 1) * tn] = jnp.dot(hb_ref[...], wdn_ref[f % 2].astype(BF),
                                                        preferred_element_type=F32)


def _experts(plan, h, w_gate, w_up, w_down, bm, nblk):
    tt, d = h.shape
    de = w_gate.shape[-1]
    tka = _pick(d, (512, 256, 128))
    tn = _pick(d, (1024, 512, 256, 128))
    nka = d // tka
    nkb = d // tn
    assert tka % MOE_DMA_SPLIT == 0 and de % MOE_DMA_SPLIT == 0
    hbm = pl.BlockSpec(memory_space=pl.ANY)
    grid_spec = pltpu.PrefetchScalarGridSpec(
        num_scalar_prefetch=4,
        grid=(nblk, nka + nkb),
        in_specs=[hbm, hbm, hbm, hbm],
        out_specs=pl.BlockSpec((bm, d), lambda j, k, be, br, bs, tk: (j, 0)),
        scratch_shapes=[pltpu.VMEM((bm, d), F32), pltpu.VMEM((bm, d), BF),
                        pltpu.VMEM((bm, de), F32), pltpu.VMEM((bm, de), F32), pltpu.VMEM((bm, de), BF),
                        pltpu.VMEM((2, 2, tka, de), F32), pltpu.VMEM((2, de, tn), F32),
                        pltpu.SemaphoreType.DMA(()), pltpu.SemaphoreType.DMA((2,)), pltpu.SemaphoreType.DMA((2,))],
    )
    return pl.pallas_call(
        functools.partial(_experts_kernel, bm=bm, nka=nka, nkb=nkb, tka=tka, tn=tn),
        grid_spec=grid_spec,
        out_shape=jax.ShapeDtypeStruct((nblk * bm, d), F32),
        compiler_params=_cparams(2),
        name="moe_experts",
    )(plan["b_exp"], plan["b_rows"], plan["b_src"], plan["tok_sorted"], h, w_gate, w_up, w_down)


def _moe_plan(e1, e2, bm, nblk):
    tt = e1.shape[0]
    na = 2 * tt
    ids = jnp.arange(N_EXPERTS, dtype=I32)[None, :]

    def lookup(table, idx):
        return jnp.sum(jnp.where(idx[:, None] == ids, table[None, :], 0), axis=1)

    e_flat = jnp.stack([e1, e2], axis=1).reshape(na)
    iota = jnp.arange(na, dtype=I32)
    e_sorted, order = lax.sort((e_flat, iota), num_keys=1, is_stable=True)
    counts = jnp.sum((e_flat[:, None] == ids).astype(I32), axis=0)
    nb_e = (counts + bm - 1) // bm
    blk_end = jnp.cumsum(nb_e)
    blk_start = blk_end - nb_e
    start = jnp.cumsum(counts) - counts
    dest = lookup(blk_start, e_sorted) * bm + iota - lookup(start, e_sorted)
    _, pos = lax.sort((order, dest), num_keys=1)
    j = jnp.arange(nblk, dtype=I32)
    n_valid = blk_end[-1]
    jj = jnp.minimum(j, n_valid - 1)
    b_exp = jnp.minimum(jnp.sum((blk_end[None, :] <= jj[:, None]).astype(I32), axis=1), N_EXPERTS - 1)
    first = (jj - lookup(blk_start, b_exp)) * bm
    b_rows = jnp.where(j < n_valid, jnp.clip(lookup(counts, b_exp) - first, 0, bm), 0)
    b_src = lookup(start, b_exp) + first
    return {"tok_sorted": order // 2, "pos": pos, "b_exp": b_exp, "b_rows": b_rows.astype(I32),
            "b_src": b_src.astype(I32)}


def _combine_ln_kernel(pos_ref, y_hbm, gate_ref, h_ref, g_ref, b_ref, op_ref, os_ref, r_ref, sem_ref,
                       *, tm, npt, alpha):
    i = pl.program_id(0)
    slot = lax.rem(i, 2)

    def gather(step, dst_slot, start):
        def body(r, carry):
            a = 2 * (step * tm + r)
            for pick in range(2):
                src = y_hbm.at[pl.ds(pos_ref[a + pick] if start else 0, 1)]
                cp = pltpu.make_async_copy(src, r_ref.at[dst_slot, pick, pl.ds(r, 1)], sem_ref.at[dst_slot])
                if start:
                    cp.start()
                else:
                    cp.wait()
            return carry

        lax.fori_loop(0, tm, body, 0, unroll=8)

    @pl.when(i == 0)
    def _():
        gather(i, slot, True)

    @pl.when(i + 1 < pl.num_programs(0))
    def _():
        gather(i + 1, 1 - slot, True)

    gather(i, slot, False)
    gate = gate_ref[...]

    y = gate[:, 0:1] * r_ref[slot, 0] + gate[:, 1:2] * r_ref[slot, 1]
    out = _layernorm(alpha * h_ref[...] + y, g_ref[...], b_ref[...])

    @pl.when(i < npt)
    def _():
        op_ref[...] = out

    @pl.when(i >= npt)
    def _():
        os_ref[...] = out


def _combine_ln(pos, y_sorted, gates, h, g, b, alpha, tp):
    tt, d = h.shape
    tm = _pick(math.gcd(tp, tt - tp), (128,))
    npt = tp // tm
    grid_spec = pltpu.PrefetchScalarGridSpec(
        num_scalar_prefetch=1,
        grid=(tt // tm,),
        in_specs=[pl.BlockSpec(memory_space=pl.ANY),
                  pl.BlockSpec((tm, LANE), lambda i, p: (i, 0)),
                  pl.BlockSpec((tm, d), lambda i, p: (i, 0)),
                  pl.BlockSpec((1, d), lambda i, p: (0, 0)),
                  pl.BlockSpec((1, d), lambda i, p: (0, 0))],
        out_specs=[pl.BlockSpec((tm, d), lambda i, p: (jnp.minimum(i, npt - 1), 0)),
                   pl.BlockSpec((tm, d), lambda i, p: (jnp.maximum(i - npt, 0), 0))],
        scratch_shapes=[pltpu.VMEM((2, 2, tm, d), F32), pltpu.SemaphoreType.DMA((2,))],
    )
    return pl.pallas_call(
        functools.partial(_combine_ln_kernel, tm=tm, npt=npt, alpha=alpha),
        grid_spec=grid_spec,
        out_shape=[jax.ShapeDtypeStruct((tp, d), F32), jax.ShapeDtypeStruct((tt - tp, d), F32)],
        compiler_params=_cparams(1),
        name="moe_combine_ln2",
    )(pos, y_sorted, gates, h, g, b)


def _rotate_half_cols(w):
    half = MLA_ROPE // 2
    return jnp.concatenate([-w[..., half:], w[..., :half]], axis=-1)


def _prep_w_in_t(w_in):
    wt = jnp.transpose(w_in)
    offs = [0]
    for s in (Q_RANK, KV_RANK, MLA_ROPE, DSA_HEADS * DSA_DIM, DSA_DIM, DSA_DIM, IDX_HEADS * IDX_DIM, IDX_DIM, IDX_HEADS):
        offs.append(offs[-1] + s)
    c_q, c_kv, k_pe, q_b, k_b, v_b, q_i, k_i, w_i = [wt[offs[n]:offs[n + 1]] for n in range(9)]
    half = MLA_ROPE // 2
    k_rot = jnp.concatenate([-k_pe[half:], k_pe[:half]], axis=0)
    pad = jnp.zeros((LANE - IDX_HEADS, wt.shape[1]), wt.dtype)
    return jnp.concatenate([q_i, q_b, c_kv, k_b, v_b, c_q, k_i, k_pe, k_rot, w_i, pad], axis=0).astype(BF)


def _lookup_rows(table, idx):
    onehot = (idx.reshape(-1, 1) == jnp.arange(table.shape[0], dtype=I32)[None, :]).astype(F32)
    out = jnp.dot(onehot, table.astype(F32), precision=lax.Precision.HIGHEST)
    return out.reshape(idx.shape + table.shape[1:])


def _prep_w_uq(w_uq):
    w = w_uq.reshape(Q_RANK, MLA_HEADS, MLA_NOPE + MLA_ROPE)
    w_nope = w[:, :, :MLA_NOPE].reshape(Q_RANK, MLA_HEADS * MLA_NOPE)
    pe = w[:, :, MLA_NOPE:]
    w_pr = jnp.concatenate([pe, _rotate_half_cols(pe)], axis=-1).reshape(Q_RANK, MLA_HEADS * 2 * MLA_ROPE)
    return w_nope.astype(BF), w_pr.astype(BF)


def _rope_table(pos):
    half = MLA_ROPE // 2
    inv = ROPE_THETA ** (-jnp.arange(half, dtype=F32) / half)
    ang = pos.astype(F32)[:, None] * inv[None, :]
    cos, sin = jnp.cos(ang), jnp.sin(ang)
    return jnp.concatenate([cos, cos, sin, sin], axis=-1)


def _bias_by_distance(rel_bias, n):
    dist = jnp.arange(n, dtype=I32)
    max_exact = NUM_BUCKETS // 2
    far = max_exact + (jnp.log(jnp.maximum(dist, 1).astype(F32) / max_exact)
                       / math.log(MAX_DISTANCE / max_exact) * (NUM_BUCKETS - max_exact)).astype(I32)
    bucket = jnp.where(dist < max_exact, dist, jnp.minimum(far, NUM_BUCKETS - 1))
    return rel_bias[bucket]


def kernel(x_prompt, x_sample, cache_latent, cache_kv, cache_idx, page_table, rel_bias, w_in, g_q, g_kv, w_uq, w_uk, w_uv, w_o, ln1_g, ln1_b, w_group, b_group, w_router, b_router, w_gate, w_up, w_down, ln2_g, ln2_b):
    depth = w_in.shape[0]
    assert depth == 1
    nbatch, seq, d_model = x_prompt.shape
    nseq, ntok, _ = x_sample.shape
    npages = page_table.shape[1]
    past = npages * PAGE_SIZE
    tp = nbatch * seq
    ts = nseq * ntok
    alpha = (2.0 * depth) ** 0.25
    assert seq % QBLK == 0 and ts % QBLK == 0 and QBLK % ntok == 0 and 4 <= ntok <= 8

    w_in_t = _prep_w_in_t(w_in[0])
    w_nope, w_pr = _prep_w_uq(w_uq[0])
    w_ukt = jnp.transpose(w_uk[0], (1, 2, 0)).astype(BF)
    w_uvt = jnp.transpose(w_uv[0], (1, 0, 2)).astype(BF)
    w_o_b = w_o[0].astype(BF)
    w_rt = jnp.concatenate([w_group[0], w_router[0],
                            jnp.zeros((d_model, LANE - N_GROUPS - N_EXPERTS), F32)], axis=1)
    w_rt_hi = w_rt.astype(BF)
    w_rt_lo = (w_rt - w_rt_hi.astype(F32)).astype(BF)
    b_rt = jnp.concatenate([b_group[0], b_router[0], jnp.zeros((LANE - N_GROUPS - N_EXPERTS,), F32)])[None, :]

    cs_p = _rope_table(jnp.tile(jnp.arange(seq, dtype=I32), nbatch))
    cs_s = _rope_table(jnp.tile(past + jnp.arange(ntok, dtype=I32), nseq))

    tab = _bias_by_distance(rel_bias, 2 * QBLK)
    far = rel_bias[NUM_BUCKETS - 1]
    tabc = (tab - far[None, :]).astype(F32)
    r = jnp.arange(QBLK, dtype=I32)[:, None]
    cc = jnp.arange(2 * QBLK, dtype=I32)[None, :]
    bt = jnp.transpose(_lookup_rows(tabc, jnp.clip(QBLK + r - cc, 0, 2 * QBLK - 1)), (2, 0, 1))
    tq = jnp.repeat(jnp.arange(ntok, dtype=I32), DSA_HEADS)[:, None]
    hq = jnp.tile(jnp.arange(DSA_HEADS, dtype=I32), ntok)[:, None, None]
    cl = jnp.arange(LANE, dtype=I32)[None, :]
    own_head = hq == jnp.arange(DSA_HEADS, dtype=I32)[None, None, :]

    def per_row_head(idx):
        return jnp.sum(jnp.where(own_head, _lookup_rows(tabc, idx), 0.0), axis=-1)

    b_last = per_row_head(jnp.clip(PAGE_SIZE + tq - cl, 0, 2 * QBLK - 1))
    b_new = jnp.where(cl <= tq, per_row_head(jnp.clip(tq - cl, 0, 2 * QBLK - 1)), 0.0)

    x_p = x_prompt.reshape(tp, d_model)
    x_s = x_sample.reshape(ts, d_model)
    hproj = _matmul(x_p, w_in_t, F32)
    hproj_s = _matmul(x_s, w_in_t, F32)
    q_cat_p = _qpath(hproj, g_q, w_nope, w_pr, w_ukt, cs_p)
    q_cat_s = _qpath(hproj_s, g_q, w_nope, w_pr, w_ukt, cs_s)
    lat_p, kvo_p, kio_p, lat_b, kv_b, ki_b = _kvpath(hproj, g_kv, cs_p)
    lat_s, kvo_s, kio_s, lat_bs, kv_bs, ki_bs = _kvpath(hproj_s, g_kv, cs_s)

    o_lat_p = _mla_prompt(q_cat_p, lat_b.reshape(nbatch, seq, LATENT_DIM))
    topk_p = min(IDX_TOPK_MAX, seq // 4)
    o_b_p = _dsa_prompt(hproj, ki_b.reshape(nbatch, seq, IDX_DIM), kv_b.reshape(nbatch, seq, 2 * DSA_DIM),
                        bt, nbatch, seq, topk_p)

    pg_lat = _pick(npages, (32, 16, 8, 4, 2, 1))
    pg_kv = _pick(npages, (32, 16, 8, 4, 2, 1))
    pg_idx = _pick(npages, (64, 32, 16, 8, 4, 2, 1))
    nbs = ts // QBLK
    spb = QBLK // ntok
    q_cat_s = q_cat_s.reshape(nbs, MLA_HEADS, spb, ntok, LATENT_DIM)
    q_cat_s = jnp.transpose(q_cat_s, (0, 2, 3, 1, 4)).reshape(nseq, ntok * MLA_HEADS, LATENT_DIM)

    def new_rows(a):
        a = a.reshape(nseq, ntok, a.shape[-1])
        return jnp.pad(a, ((0, 0), (0, LANE - ntok), (0, 0)))

    cache_latent_t = jnp.swapaxes(cache_latent, 2, 3)
    o_lat_s = _mla_sample(page_table, q_cat_s, cache_latent_t, new_rows(lat_bs), pg_lat)
    o_lat_s = o_lat_s.reshape(nbs, spb, ntok, MLA_HEADS, KV_RANK)
    o_lat_s = jnp.transpose(o_lat_s, (0, 3, 1, 2, 4)).reshape(nbs, MLA_HEADS, QBLK, KV_RANK)

    q_i_s = hproj_s[:, C_QI:C_QI + IDX_HEADS * IDX_DIM].astype(BF).reshape(nseq, ntok * IDX_HEADS, IDX_DIM)
    w_i_s = hproj_s[:, C_WI:C_WI + IDX_HEADS].reshape(nseq, ntok * IDX_HEADS, 1)
    s_past, s_new = _idx_sample(page_table, q_i_s, w_i_s, cache_idx, new_rows(ki_bs), pg_idx, ntok)
    topk_s = min(IDX_TOPK_MAX, (past + ntok) // 4)
    thr, cut = _topk_sample(s_past.reshape(ts, past), s_new.reshape(ts, LANE), topk_s)
    q_b_s = hproj_s[:, C_QB:C_QB + DSA_HEADS * DSA_DIM].astype(BF).reshape(nseq, ntok * DSA_HEADS, DSA_DIM)
    o_b_s = _dsa_sample(page_table, q_b_s, s_past, s_new, thr.reshape(nseq, ntok, LANE),
                        cut.reshape(nseq, ntok, LANE), cache_kv, new_rows(kv_bs), b_last, b_new, pg_kv, ntok)
    o_b_s = o_b_s.reshape(ts, DSA_HEADS * DSA_DIM)

    h1 = _wo_ln(_uv(o_lat_p, w_uvt), o_b_p, _uv(o_lat_s, w_uvt), o_b_s, w_o_b, x_p, x_s, ln1_g, ln1_b, alpha)

    r_idx, r_gate = _router(h1, w_rt_hi, w_rt_lo, b_rt)
    tt = tp + ts
    bm = MOE_BM
    nblk = (2 * tt) // bm + N_EXPERTS
    plan = _moe_plan(r_idx[:, 0], r_idx[:, 1], bm, nblk)
    y_sorted = _experts(plan, h1, w_gate, w_up, w_down, bm, nblk)
    out_p, out_s = _combine_ln(plan["pos"], y_sorted, r_gate, h1, ln2_g, ln2_b, alpha, tp)

    kvw = 2 * DSA_DIM
    return (out_p.reshape(nbatch, seq, d_model), out_s.reshape(nseq, ntok, d_model),
            lat_p.reshape(1, nbatch, seq, LATENT_DIM), kvo_p.reshape(1, nbatch, seq, kvw),
            kio_p.reshape(1, nbatch, seq, IDX_DIM),
            lat_s.reshape(1, nseq, ntok, LATENT_DIM), kvo_s.reshape(1, nseq, ntok, kvw),
            kio_s.reshape(1, nseq, ntok, IDX_DIM))
```

```python
import functools
import math

import jax
import jax.numpy as jnp
from jax import lax
from jax.experimental import pallas as pl
from jax.experimental.pallas import tpu as pltpu

BF = jnp.bfloat16
F32 = jnp.float32
I32 = jnp.int32

MLA_HEADS = 16
MLA_NOPE = 128
MLA_ROPE = 64
MLA_V = 128
Q_RANK = 768
KV_RANK = 512
LATENT_DIM = KV_RANK + MLA_ROPE
ROPE_THETA = 10000.0
MLA_SCALE = (MLA_NOPE + MLA_ROPE) ** -0.5
DSA_HEADS = 16
DSA_DIM = 128
DSA_SCALE = DSA_DIM ** -0.5
IDX_HEADS = 32
IDX_DIM = 128
IDX_SCALE = IDX_DIM ** -0.5
IDX_TOPK_MAX = 256
NUM_BUCKETS = 32
MAX_DISTANCE = 128
N_GROUPS = 8
EXPERTS_PER_GROUP = 8
N_EXPERTS = N_GROUPS * EXPERTS_PER_GROUP
PAGE_SIZE = 128
LN_EPS = 1e-5
RMS_EPS = 1e-6

LANE = 128
QBLK = 128
NEG = -1e30
VMEM_LIMIT = 56 * 1024 * 1024

C_QI = 0
C_QB = C_QI + IDX_HEADS * IDX_DIM
C_CKV = C_QB + DSA_HEADS * DSA_DIM
C_KV = C_CKV + KV_RANK
C_CQ = C_KV + 2 * DSA_DIM
C_KI = C_CQ + Q_RANK
C_KPE = C_KI + IDX_DIM
C_WI = C_KPE + 2 * MLA_ROPE
D_PROJ = C_WI + LANE
assert C_QB % (DSA_HEADS * DSA_DIM) == 0 and C_CKV % KV_RANK == 0 and C_KV % (2 * DSA_DIM) == 0
assert C_CQ % Q_RANK == 0 and C_KI % LANE == 0


def _cparams(n_axes, vmem=VMEM_LIMIT):
    return pltpu.CompilerParams(dimension_semantics=("arbitrary",) * n_axes, vmem_limit_bytes=vmem)


def _pick(n, cands):
    for c in cands:
        if n % c == 0:
            return c
    return n


def _nt_dot(a, b):
    return lax.dot_general(a, b, (((1,), (1,)), ((), ())), preferred_element_type=F32)


def _mm_kernel(x_ref, w_ref, o_ref, xb_ref):
    @pl.when(pl.program_id(1) == 0)
    def _():
        xb_ref[...] = x_ref[...].astype(BF)

    o_ref[...] = _nt_dot(xb_ref[...], w_ref[...]).astype(o_ref.dtype)


def _matmul(x, wt, out_dtype):
    m, k = x.shape
    n = wt.shape[0]
    tm = _pick(m, (512, 256, 128))
    tn = _pick(n, (1152, 1024, 896, 512, 384, 256, 128))
    return pl.pallas_call(
        _mm_kernel,
        grid=(m // tm, n // tn),
        in_specs=[pl.BlockSpec((tm, k), lambda i, j: (i, 0)),
                  pl.BlockSpec((tn, k), lambda i, j: (j, 0))],
        out_specs=pl.BlockSpec((tm, tn), lambda i, j: (i, j)),
        out_shape=jax.ShapeDtypeStruct((m, n), out_dtype),
        scratch_shapes=[pltpu.VMEM((tm, k), BF)],
        compiler_params=_cparams(2),
        name="in_proj",
    )(x, wt)


def _qpath_kernel(cq_ref, g_ref, wn_ref, wp_ref, wuk_ref, cs_ref, o_ref):
    x = cq_ref[...]
    y = x * lax.rsqrt(jnp.mean(x * x, axis=-1, keepdims=True) + RMS_EPS) * g_ref[...]
    yb = y.astype(BF)
    a = jnp.dot(yb, wn_ref[...], preferred_element_type=F32)
    b = jnp.dot(yb, wp_ref[...], preferred_element_type=F32)
    cs = cs_ref[...]
    for h in range(MLA_HEADS):
        qn = a[:, h * LANE:(h + 1) * LANE].astype(BF)
        ql = jnp.dot(qn, wuk_ref[h], preferred_element_type=F32) * MLA_SCALE
        pe = b[:, h * LANE:(h + 1) * LANE] * cs
        pe = (pe + pltpu.roll(pe, MLA_ROPE, 1)) * MLA_SCALE
        o_ref[0, h, :, 0:KV_RANK] = ql.astype(BF)
        o_ref[0, h, :, KV_RANK:LATENT_DIM] = pe[:, 0:MLA_ROPE].astype(BF)


def _qpath(hproj, g_q, w_nope, w_pr, w_ukt, cs):
    tt = hproj.shape[0]
    nb = tt // QBLK
    hw = MLA_HEADS * LANE
    return pl.pallas_call(
        _qpath_kernel,
        grid=(nb,),
        in_specs=[pl.BlockSpec((QBLK, Q_RANK), lambda i: (i, C_CQ // Q_RANK)),
                  pl.BlockSpec((1, Q_RANK), lambda i: (0, 0)),
                  pl.BlockSpec((Q_RANK, hw), lambda i: (0, 0)),
                  pl.BlockSpec((Q_RANK, hw), lambda i: (0, 0)),
                  pl.BlockSpec((MLA_HEADS, MLA_NOPE, KV_RANK), lambda i: (0, 0, 0)),
                  pl.BlockSpec((QBLK, LANE), lambda i: (i, 0))],
        out_specs=pl.BlockSpec((1, MLA_HEADS, QBLK, LATENT_DIM), lambda i: (i, 0, 0, 0)),
        out_shape=jax.ShapeDtypeStruct((nb, MLA_HEADS, QBLK, LATENT_DIM), BF),
        compiler_params=_cparams(1),
        name="mla_qpath",
    )(hproj, g_q, w_nope, w_pr, w_ukt, cs)


def _kvpath_kernel(ckv_ref, kpe_ref, kv_ref, ki_ref, g_ref, cs_ref,
                   lat_ref, kvo_ref, kio_ref, latb_ref, kvb_ref, kib_ref):
    x = ckv_ref[...]
    c = x * lax.rsqrt(jnp.mean(x * x, axis=-1, keepdims=True) + RMS_EPS) * g_ref[...]
    pe = kpe_ref[...] * cs_ref[...]
    pe = (pe + pltpu.roll(pe, MLA_ROPE, 1))[:, 0:MLA_ROPE]
    lat_ref[:, 0:KV_RANK] = c
    lat_ref[:, KV_RANK:LATENT_DIM] = pe
    latb_ref[:, 0:KV_RANK] = c.astype(BF)
    latb_ref[:, KV_RANK:LATENT_DIM] = pe.astype(BF)
    kv = kv_ref[...]
    kvo_ref[...] = kv
    kvb_ref[...] = kv.astype(BF)
    ki = ki_ref[...]
    kio_ref[...] = ki
    kib_ref[...] = ki.astype(BF)


def _kvpath(hproj, g_kv, cs):
    tt = hproj.shape[0]
    tm = _pick(tt, (256, 128))
    kvw = 2 * DSA_DIM
    outs = [jax.ShapeDtypeStruct((tt, LATENT_DIM), F32), jax.ShapeDtypeStruct((tt, kvw), F32),
            jax.ShapeDtypeStruct((tt, IDX_DIM), F32), jax.ShapeDtypeStruct((tt, LATENT_DIM), BF),
            jax.ShapeDtypeStruct((tt, kvw), BF), jax.ShapeDtypeStruct((tt, IDX_DIM), BF)]
    ospec = [pl.BlockSpec((tm, LATENT_DIM), lambda i: (i, 0)), pl.BlockSpec((tm, kvw), lambda i: (i, 0)),
             pl.BlockSpec((tm, IDX_DIM), lambda i: (i, 0))]
    return pl.pallas_call(
        _kvpath_kernel,
        grid=(tt // tm,),
        in_specs=[pl.BlockSpec((tm, KV_RANK), lambda i: (i, C_CKV // KV_RANK)),
                  pl.BlockSpec((tm, LANE), lambda i: (i, C_KPE // LANE)),
                  pl.BlockSpec((tm, kvw), lambda i: (i, C_KV // kvw)),
                  pl.BlockSpec((tm, IDX_DIM), lambda i: (i, C_KI // IDX_DIM)),
                  pl.BlockSpec((1, KV_RANK), lambda i: (0, 0)),
                  pl.BlockSpec((tm, LANE), lambda i: (i, 0))],
        out_specs=ospec + ospec,
        out_shape=outs,
        compiler_params=_cparams(1),
        name="kv_path",
    )(hproj, hproj, hproj, hproj, g_kv, cs)


def _mla_prompt_kernel(q_ref, lat_ref, o_ref, m_ref, l_ref, acc_ref, *, kc):
    i = pl.program_id(1)
    rows = MLA_HEADS * QBLK
    q = q_ref[0].reshape(rows, LATENT_DIM)
    m_ref[...] = jnp.full(m_ref.shape, NEG, F32)
    l_ref[...] = jnp.zeros(l_ref.shape, F32)
    acc_ref[...] = jnp.zeros(acc_ref.shape, F32)
    row_t = i * QBLK + (lax.broadcasted_iota(I32, (rows, kc), 0) & (QBLK - 1))
    col = lax.broadcasted_iota(I32, (rows, kc), 1)
    n_chunks = ((i + 1) * QBLK + kc - 1) // kc

    def body(c, carry):
        off = pl.multiple_of(c * kc, kc)
        k = lat_ref[0, pl.ds(off, kc), :]
        s = _nt_dot(q, k)
        s = jnp.where(col + off <= row_t, s, NEG)
        m_old = m_ref[...]
        m_new = jnp.maximum(m_old, jnp.max(s, axis=-1, keepdims=True))
        a = jnp.exp(m_old - m_new)
        p = jnp.exp(s - m_new)
        l_ref[...] = a * l_ref[...] + jnp.sum(p, axis=-1, keepdims=True)
        acc_ref[...] = a * acc_ref[...] + jnp.dot(p.astype(BF), k[:, 0:KV_RANK], preferred_element_type=F32)
        m_ref[...] = m_new
        return carry

    lax.fori_loop(0, n_chunks, body, 0)
    o = acc_ref[...] / l_ref[...]
    o_ref[0] = o.reshape(MLA_HEADS, QBLK, KV_RANK).astype(o_ref.dtype)


def _mla_prompt(q_cat, lat_b):
    nbatch, seq, _ = lat_b.shape
    nbq = seq // QBLK
    kc = _pick(seq, (512, 256, 128))
    rows = MLA_HEADS * QBLK
    return pl.pallas_call(
        functools.partial(_mla_prompt_kernel, kc=kc),
        grid=(nbatch, nbq),
        in_specs=[pl.BlockSpec((1, MLA_HEADS, QBLK, LATENT_DIM), lambda b, i: (b * nbq + i, 0, 0, 0)),
                  pl.BlockSpec((1, seq, LATENT_DIM), lambda b, i: (b, 0, 0))],
        out_specs=pl.BlockSpec((1, MLA_HEADS, QBLK, KV_RANK), lambda b, i: (b * nbq + i, 0, 0, 0)),
        out_shape=jax.ShapeDtypeStruct((nbatch * nbq, MLA_HEADS, QBLK, KV_RANK), BF),
        scratch_shapes=[pltpu.VMEM((rows, 1), F32), pltpu.VMEM((rows, 1), F32), pltpu.VMEM((rows, KV_RANK), F32)],
        compiler_params=_cparams(2),
        name="mla_prompt",
    )(q_cat, lat_b)


def _uv_kernel(o_ref, w_ref, out_ref):
    for h in range(MLA_HEADS):
        out_ref[:, h * MLA_V:(h + 1) * MLA_V] = jnp.dot(
            o_ref[0, h], w_ref[h], preferred_element_type=F32).astype(out_ref.dtype)


def _uv(o_lat, w_uvt):
    nb = o_lat.shape[0]
    return pl.pallas_call(
        _uv_kernel,
        grid=(nb,),
        in_specs=[pl.BlockSpec((1, MLA_HEADS, QBLK, KV_RANK), lambda i: (i, 0, 0, 0)),
                  pl.BlockSpec((MLA_HEADS, KV_RANK, MLA_V), lambda i: (0, 0, 0))],
        out_specs=pl.BlockSpec((QBLK, MLA_HEADS * MLA_V), lambda i: (i, 0)),
        out_shape=jax.ShapeDtypeStruct((nb * QBLK, MLA_HEADS * MLA_V), BF),
        compiler_params=_cparams(1),
        name="mla_uv",
    )(o_lat, w_uvt)


def _float_key(x):
    bits = pltpu.bitcast(x, I32)
    return jnp.where(bits < 0, bits ^ jnp.int32(0x7FFFFFFF), bits)


def _topk_threshold(key_ref, k):
    nrows, ncols = key_ref.shape
    idx_bits = int(math.ceil(math.log2(ncols))) + 1
    kf = jnp.float32(k)

    def count(pred):
        return jnp.sum(jnp.where(pred, 1.0, 0.0), axis=-1, keepdims=True)

    def vbody(b, thr):
        cand = thr + lax.shift_left(jnp.int32(1), 31 - b)
        return jnp.where(count(key_ref[...] >= cand) >= kf, cand, thr)

    thr = lax.fori_loop(0, 32, vbody, jnp.full((nrows, 1), -2 ** 31, I32))
    key = key_ref[...]
    need = kf - count(key > thr)
    some_row_cuts = jnp.max(jnp.where(count(key == thr) == need, 0.0, 1.0)) > 0.0

    def cbody(b, cut):
        cand = cut + lax.shift_left(jnp.int32(1), idx_bits - 1 - b)
        col = lax.broadcasted_iota(I32, (nrows, ncols), 1)
        n = jnp.sum(jnp.where(key_ref[...] == thr, jnp.where(col < cand, 1.0, 0.0), 0.0), axis=-1, keepdims=True)
        return jnp.where(n <= need, cand, cut)

    cut0 = jnp.where(some_row_cuts, jnp.zeros((nrows, 1), I32), jnp.full((nrows, 1), 2 ** idx_bits, I32))
    cut = lax.fori_loop(0, jnp.where(some_row_cuts, idx_bits, 0), cbody, cut0)
    return thr, cut


def _selected(key, col, thr, cut):
    return (key > thr) | ((key == thr) & (col < cut))


def _dsa_prompt_kernel(qi_ref, wi_ref, qb_ref, ki_ref, kv_ref, bt_ref, o_ref,
                       qa_ref, wb_ref, sc_ref, key_ref, lg_ref, *, topk, kch):
    i = pl.program_id(1)
    seq = ki_ref.shape[1]
    w = wi_ref[...] * (IDX_SCALE * IDX_HEADS ** -0.5)
    lane = lax.broadcasted_iota(I32, (QBLK, LANE), 1)
    for hh in range(IDX_HEADS):
        qa_ref[hh * QBLK:(hh + 1) * QBLK, :] = qi_ref[:, hh * IDX_DIM:(hh + 1) * IDX_DIM].astype(BF)
        wcol = jnp.sum(jnp.where(lane == hh, w, 0.0), axis=-1, keepdims=True)
        wb_ref[hh] = jnp.broadcast_to(wcol, (QBLK, LANE))
    off = pl.multiple_of(i * QBLK, QBLK)

    def body(kend):
        for ch in range(kend // kch):
            d = _nt_dot(qa_ref[...], ki_ref[0, ch * kch:(ch + 1) * kch, :])
            acc = None
            for hh in range(IDX_HEADS):
                wt = jnp.concatenate([wb_ref[hh]] * (kch // LANE), axis=1)
                term = jnp.maximum(d[hh * QBLK:(hh + 1) * QBLK], 0.0) * wt
                acc = term if acc is None else acc + term
            sc_ref[:, ch * kch:(ch + 1) * kch] = acc

        t_idx = i * QBLK + lax.broadcasted_iota(I32, (QBLK, kend), 0)
        s_idx = lax.broadcasted_iota(I32, (QBLK, kend), 1)
        causal = s_idx <= t_idx
        key_ref[:, 0:kend] = _float_key(jnp.where(causal, sc_ref[:, 0:kend], -jnp.inf))
        thr, cut = _topk_threshold(key_ref.at[:, 0:kend], topk)
        sc_ref[:, 0:kend] = jnp.where(_selected(key_ref[:, 0:kend], s_idx, thr, cut) & causal, 0.0, NEG)

        def abody(h, carry):
            hoff = pl.multiple_of(h * DSA_DIM, DSA_DIM)
            q = qb_ref[:, pl.ds(hoff, DSA_DIM)].astype(BF)
            lg_ref[:, 0:kend] = _nt_dot(q, kv_ref[0, 0:kend, 0:DSA_DIM]) * DSA_SCALE + sc_ref[:, 0:kend]
            lg_ref[:, pl.ds(off, QBLK)] += bt_ref[h, :, QBLK:2 * QBLK]

            @pl.when(i >= 1)
            def _():
                lg_ref[:, pl.ds(off - QBLK, QBLK)] += bt_ref[h, :, 0:QBLK]

            lg = lg_ref[:, 0:kend]
            m = jnp.max(lg, axis=-1, keepdims=True)
            p = jnp.exp(lg - m)
            l = jnp.sum(p, axis=-1, keepdims=True)
            o = jnp.dot(p.astype(BF), kv_ref[0, 0:kend, DSA_DIM:2 * DSA_DIM], preferred_element_type=F32) / l
            o_ref[:, pl.ds(hoff, DSA_DIM)] = o.astype(o_ref.dtype)
            return carry

        lax.fori_loop(0, DSA_HEADS, abody, 0)

    n_ch = (off + QBLK + kch - 1) // kch
    for n in range(1, seq // kch + 1):
        pl.when(n_ch == n)(functools.partial(body, n * kch))


def _dsa_prompt(hproj, ki_b, kv_b, bt, nbatch, seq, topk):
    nbq = seq // QBLK
    qiw = IDX_HEADS * IDX_DIM
    qbw = DSA_HEADS * DSA_DIM
    kch = _pick(seq, (512, 256, 128))
    assert kch >= topk
    return pl.pallas_call(
        functools.partial(_dsa_prompt_kernel, topk=topk, kch=kch),
        grid=(nbatch, nbq),
        in_specs=[pl.BlockSpec((QBLK, qiw), lambda b, i: (b * nbq + i, C_QI // qiw)),
                  pl.BlockSpec((QBLK, LANE), lambda b, i: (b * nbq + i, C_WI // LANE)),
                  pl.BlockSpec((QBLK, qbw), lambda b, i: (b * nbq + i, C_QB // qbw)),
                  pl.BlockSpec((1, seq, IDX_DIM), lambda b, i: (b, 0, 0)),
                  pl.BlockSpec((1, seq, 2 * DSA_DIM), lambda b, i: (b, 0, 0)),
                  pl.BlockSpec((DSA_HEADS, QBLK, 2 * QBLK), lambda b, i: (0, 0, 0))],
        out_specs=pl.BlockSpec((QBLK, qbw), lambda b, i: (b * nbq + i, 0)),
        out_shape=jax.ShapeDtypeStruct((nbatch * seq, qbw), BF),
        scratch_shapes=[pltpu.VMEM((IDX_HEADS * QBLK, IDX_DIM), BF), pltpu.VMEM((IDX_HEADS, QBLK, LANE), F32),
                        pltpu.VMEM((QBLK, seq), F32), pltpu.VMEM((QBLK, seq), I32), pltpu.VMEM((QBLK, seq), F32)],
        compiler_params=_cparams(2),
        name="dsa_prompt",
    )(hproj, hproj, hproj, ki_b, kv_b, bt)


PAGE_GROUP = 8


PAGE_SLOTS = 3


def _page_pipeline(pt_ref, cache_hbm, buf_ref, sem_ref, pg):
    nc = pl.num_programs(1)
    total = pl.num_programs(0) * nc
    t = pl.program_id(0) * nc + pl.program_id(1)
    slot = lax.rem(t, PAGE_SLOTS)

    def page_copy(page, dst_slot, j):
        return pltpu.make_async_copy(cache_hbm.at[0, page], buf_ref.at[dst_slot, j], sem_ref.at[dst_slot])

    def start(step, dst_slot):
        s = step // nc
        c = lax.rem(step, nc)
        for j in range(pg):
            page_copy(pt_ref[s, c * pg + j], dst_slot, j).start()

    @pl.when(t == 0)
    def _():
        start(t, slot)

        @pl.when(1 < total)
        def _():
            start(t + 1, 1)

    @pl.when(t + 2 < total)
    def _():
        start(t + 2, lax.rem(t + 2, PAGE_SLOTS))

    for j in range(pg):
        page_copy(0, slot, j).wait()
    return slot


def _paged_call(kernel_fn, name, page_table, cache, pg, page_shape, in_arrays, in_specs, out_shape, out_specs,
                scratch_shapes):
    nseq, npages = page_table.shape
    grid_spec = pltpu.PrefetchScalarGridSpec(
        num_scalar_prefetch=1,
        grid=(nseq, npages // pg),
        in_specs=in_specs + [pl.BlockSpec(memory_space=pl.ANY)],
        out_specs=out_specs,
        scratch_shapes=[pltpu.VMEM((PAGE_SLOTS, pg) + page_shape, cache.dtype),
                        pltpu.SemaphoreType.DMA((PAGE_SLOTS,))]
        + scratch_shapes,
    )
    return pl.pallas_call(kernel_fn, grid_spec=grid_spec, out_shape=out_shape, compiler_params=_cparams(2),
                          name=name)(page_table, *in_arrays, cache)


def _group_scores(d, wcol, ntok):
    x = jnp.maximum(d, 0.0) * wcol
    return jnp.concatenate(
        [jnp.sum(x[t * IDX_HEADS:(t + 1) * IDX_HEADS], axis=0, keepdims=True) for t in range(ntok)], axis=0)


def _idx_sample_kernel(pt_ref, q_ref, w_ref, knew_ref, cache_hbm, o_ref, onew_ref, buf_ref, sem_ref, *, pg, ntok):
    slot = _page_pipeline(pt_ref, cache_hbm, buf_ref, sem_ref, pg)
    c = pl.program_id(1)
    q = q_ref[0]
    wcol = w_ref[0] * (IDX_SCALE * IDX_HEADS ** -0.5)
    grp = min(PAGE_GROUP, pg)
    for g in range(pg // grp):
        kc = jnp.concatenate([buf_ref[slot, g * grp + j].astype(BF) for j in range(grp)], axis=0)
        o_ref[0, :, g * grp * PAGE_SIZE:(g + 1) * grp * PAGE_SIZE] = _group_scores(_nt_dot(q, kc), wcol, ntok)

    @pl.when(c == pl.num_programs(1) - 1)
    def _():
        sn = _group_scores(_nt_dot(q, knew_ref[0]), wcol, ntok)
        row = lax.broadcasted_iota(I32, (ntok, LANE), 0)
        colv = lax.broadcasted_iota(I32, (ntok, LANE), 1)
        onew_ref[0] = jnp.where(colv <= row, sn, -jnp.inf)


def _idx_sample(page_table, q_idx, w_col, cache_idx, k_new, pg, ntok):
    nseq, npages = page_table.shape
    sc = pg * PAGE_SIZE
    rows = ntok * IDX_HEADS
    return _paged_call(
        functools.partial(_idx_sample_kernel, pg=pg, ntok=ntok), "idx_sample", page_table, cache_idx, pg,
        (PAGE_SIZE, IDX_DIM), [q_idx, w_col, k_new],
        [pl.BlockSpec((1, rows, IDX_DIM), lambda s, c, pt: (s, 0, 0)),
         pl.BlockSpec((1, rows, 1), lambda s, c, pt: (s, 0, 0)),
         pl.BlockSpec((1, LANE, IDX_DIM), lambda s, c, pt: (s, 0, 0))],
        [jax.ShapeDtypeStruct((nseq, ntok, npages * PAGE_SIZE), F32), jax.ShapeDtypeStruct((nseq, ntok, LANE), F32)],
        [pl.BlockSpec((1, ntok, sc), lambda s, c, pt: (s, 0, c)),
         pl.BlockSpec((1, ntok, LANE), lambda s, c, pt: (s, 0, 0))],
        [])


def _topk_sample_kernel(sp_ref, sn_ref, thr_ref, cut_ref, key_ref, *, topk):
    past = sp_ref.shape[1]
    key_ref[:, 0:past] = _float_key(sp_ref[...])
    key_ref[:, past:past + LANE] = _float_key(sn_ref[...])
    thr, cut = _topk_threshold(key_ref, topk)
    thr_ref[...] = jnp.broadcast_to(thr, thr_ref.shape)
    cut_ref[...] = jnp.broadcast_to(cut, cut_ref.shape)


def _topk_sample(s_past, s_new, topk):
    nrows, past = s_past.shape
    tr = _pick(nrows, (64, 32, 16, 8))
    return pl.pallas_call(
        functools.partial(_topk_sample_kernel, topk=topk),
        grid=(nrows // tr,),
        in_specs=[pl.BlockSpec((tr, past), lambda r: (r, 0)),
                  pl.BlockSpec((tr, LANE), lambda r: (r, 0))],
        out_specs=[pl.BlockSpec((tr, LANE), lambda r: (r, 0)),
                   pl.BlockSpec((tr, LANE), lambda r: (r, 0))],
        out_shape=[jax.ShapeDtypeStruct((nrows, LANE), I32), jax.ShapeDtypeStruct((nrows, LANE), I32)],
        scratch_shapes=[pltpu.VMEM((tr, past + LANE), I32)],
        compiler_params=_cparams(1),
        name="topk_sample",
    )(s_past, s_new)


def _flash_init(m_ref, l_ref, acc_ref):
    m_ref[...] = jnp.full(m_ref.shape, NEG, F32)
    l_ref[...] = jnp.zeros(l_ref.shape, F32)
    acc_ref[...] = jnp.zeros(acc_ref.shape, F32)


def _flash_step(s, v, m_ref, l_ref, acc_ref, v_transposed=False):
    m_old = m_ref[...]
    m_new = jnp.maximum(m_old, jnp.max(s, axis=-1, keepdims=True))
    a = jnp.exp(m_old - m_new)
    p = jnp.exp(s - m_new)
    l_ref[...] = a * l_ref[...] + jnp.sum(p, axis=-1, keepdims=True)
    pb = p.astype(BF)
    pv = _nt_dot(pb, v) if v_transposed else jnp.dot(pb, v, preferred_element_type=F32)
    acc_ref[...] = a * acc_ref[...] + pv
    m_ref[...] = m_new


def _mla_sample_kernel(pt_ref, q_ref, new_ref, cache_hbm, o_ref, buf_ref, sem_ref, m_ref, l_ref, acc_ref, kt_ref,
                       *, pg, nheads):
    slot = _page_pipeline(pt_ref, cache_hbm, buf_ref, sem_ref, pg)
    c = pl.program_id(1)

    @pl.when(c == 0)
    def _():
        _flash_init(m_ref, l_ref, acc_ref)

    q = q_ref[0]
    for j in range(pg):
        kt_ref[:, j * PAGE_SIZE:(j + 1) * PAGE_SIZE] = buf_ref[slot, j].astype(BF)
    s = jnp.dot(q, kt_ref[...], preferred_element_type=F32)
    _flash_step(s, kt_ref[0:KV_RANK, :], m_ref, l_ref, acc_ref, v_transposed=True)

    @pl.when(c == pl.num_programs(1) - 1)
    def _():
        kn = new_ref[0]
        rows = q.shape[0]
        s = _nt_dot(q, kn)
        t = lax.broadcasted_iota(I32, (rows, LANE), 0) // nheads
        colv = lax.broadcasted_iota(I32, (rows, LANE), 1)
        _flash_step(jnp.where(colv <= t, s, NEG), kn[:, 0:KV_RANK], m_ref, l_ref, acc_ref)
        o_ref[0] = (acc_ref[...] / l_ref[...]).astype(o_ref.dtype)


def _mla_sample(page_table, q_cat, cache_latent_t, lat_new, pg):
    nseq = page_table.shape[0]
    rows = q_cat.shape[1]
    return _paged_call(
        functools.partial(_mla_sample_kernel, pg=pg, nheads=MLA_HEADS), "mla_sample", page_table, cache_latent_t, pg,
        (LATENT_DIM, PAGE_SIZE), [q_cat, lat_new],
        [pl.BlockSpec((1, rows, LATENT_DIM), lambda s, c, pt: (s, 0, 0)),
         pl.BlockSpec((1, LANE, LATENT_DIM), lambda s, c, pt: (s, 0, 0))],
        jax.ShapeDtypeStruct((nseq, rows, KV_RANK), BF),
        pl.BlockSpec((1, rows, KV_RANK), lambda s, c, pt: (s, 0, 0)),
        [pltpu.VMEM((rows, 1), F32), pltpu.VMEM((rows, 1), F32), pltpu.VMEM((rows, KV_RANK), F32),
         pltpu.VMEM((LATENT_DIM, pg * PAGE_SIZE), BF)])


def _expand_rows(mask, ntok, nheads):
    return jnp.concatenate(
        [jnp.broadcast_to(mask[t:t + 1], (nheads, mask.shape[1])) for t in range(ntok)], axis=0)


def _dsa_sample_kernel(pt_ref, q_ref, sp_ref, sn_ref, thr_ref, cut_ref, new_ref, blast_ref, bnew_ref, cache_hbm,
                       o_ref, buf_ref, sem_ref, m_ref, l_ref, acc_ref, kvb_ref, *, pg, ntok):
    slot = _page_pipeline(pt_ref, cache_hbm, buf_ref, sem_ref, pg)
    c = pl.program_id(1)
    last = pl.num_programs(1) - 1
    sc = pg * PAGE_SIZE

    @pl.when(c == 0)
    def _():
        _flash_init(m_ref, l_ref, acc_ref)

    q = q_ref[0]
    thr = thr_ref[0][:, 0:1]
    cut = cut_ref[0][:, 0:1]

    def mask_rows(scores, col):
        sel = _selected(_float_key(scores), col, thr, cut)
        return _expand_rows(jnp.where(sel, 0.0, NEG), ntok, DSA_HEADS)

    near_bias = jnp.where(c == last, blast_ref[...], 0.0)
    for j in range(pg):
        kvb_ref[j * PAGE_SIZE:(j + 1) * PAGE_SIZE, :] = buf_ref[slot, j].astype(BF)
    col = c * sc + lax.broadcasted_iota(I32, (ntok, sc), 1)
    lg = _nt_dot(q, kvb_ref[:, 0:DSA_DIM]) * DSA_SCALE + mask_rows(sp_ref[0], col)
    tail = lg[:, sc - LANE:sc] + near_bias
    lg = tail if sc == LANE else jnp.concatenate([lg[:, 0:sc - LANE], tail], axis=1)
    _flash_step(lg, kvb_ref[:, DSA_DIM:2 * DSA_DIM], m_ref, l_ref, acc_ref)

    @pl.when(c == last)
    def _():
        kvn = new_ref[0]
        past = pl.num_programs(1) * sc
        coln = lax.broadcasted_iota(I32, (ntok, LANE), 1)
        valid = coln <= lax.broadcasted_iota(I32, (ntok, LANE), 0)
        madd = jnp.where(valid & _selected(_float_key(sn_ref[0]), past + coln, thr, cut), 0.0, NEG)
        ln = (_nt_dot(q, kvn[:, 0:DSA_DIM]) * DSA_SCALE + bnew_ref[...] + _expand_rows(madd, ntok, DSA_HEADS))
        _flash_step(ln, kvn[:, DSA_DIM:2 * DSA_DIM], m_ref, l_ref, acc_ref)
        o_ref[0] = (acc_ref[...] / l_ref[...]).astype(o_ref.dtype)


def _dsa_sample(page_table, q_b, s_past, s_new, thr, cut, cache_kv, kv_new, b_last, b_new, pg, ntok):
    nseq = page_table.shape[0]
    rows = q_b.shape[1]
    sc = pg * PAGE_SIZE
    per_seq = lambda s, c, pt: (s, 0, 0)
    return _paged_call(
        functools.partial(_dsa_sample_kernel, pg=pg, ntok=ntok), "dsa_sample", page_table, cache_kv, pg,
        (PAGE_SIZE, 2 * DSA_DIM), [q_b, s_past, s_new, thr, cut, kv_new, b_last, b_new],
        [pl.BlockSpec((1, rows, DSA_DIM), per_seq),
         pl.BlockSpec((1, ntok, sc), lambda s, c, pt: (s, 0, c)),
         pl.BlockSpec((1, ntok, LANE), per_seq),
         pl.BlockSpec((1, ntok, LANE), per_seq),
         pl.BlockSpec((1, ntok, LANE), per_seq),
         pl.BlockSpec((1, LANE, 2 * DSA_DIM), per_seq),
         pl.BlockSpec((rows, LANE), lambda s, c, pt: (0, 0)),
         pl.BlockSpec((rows, LANE), lambda s, c, pt: (0, 0))],
        jax.ShapeDtypeStruct((nseq, rows, DSA_DIM), BF),
        pl.BlockSpec((1, rows, DSA_DIM), per_seq),
        [pltpu.VMEM((rows, 1), F32), pltpu.VMEM((rows, 1), F32), pltpu.VMEM((rows, DSA_DIM), F32),
         pltpu.VMEM((sc, 2 * DSA_DIM), BF)])


def _layernorm(z, g, b):
    mu = jnp.mean(z, axis=-1, keepdims=True)
    zc = z - mu
    var = jnp.mean(zc * zc, axis=-1, keepdims=True)
    return zc * lax.rsqrt(var + LN_EPS) * g + b


def _wo_ln_kernel(oap_ref, obp_ref, oas_ref, obs_ref, w_ref, xp_hbm, xs_hbm, g_ref, b_ref, o_ref, x_ref, sem,
                  *, npt, nka, alpha):
    i = pl.program_id(0)
    k = pl.program_id(1)
    tm = o_ref.shape[0]

    def x_copy(is_sample):
        src, row0 = (xs_hbm, (i - npt) * tm) if is_sample else (xp_hbm, i * tm)
        return pltpu.make_async_copy(src.at[pl.ds(pl.multiple_of(row0, tm), tm)], x_ref, sem)

    for is_sample in (False, True):
        @pl.when((k == 0) & ((i >= npt) == is_sample))
        def _():
            x_copy(is_sample).start()

    @pl.when(k == 0)
    def _():
        o_ref[...] = jnp.zeros(o_ref.shape, F32)

    for lhs_ref, is_sample, is_b in ((oap_ref, False, False), (obp_ref, False, True),
                                      (oas_ref, True, False), (obs_ref, True, True)):
        @pl.when(((i >= npt) == is_sample) & ((k >= nka) == is_b))
        def _():
            o_ref[...] += jnp.dot(lhs_ref[...], w_ref[...], preferred_element_type=F32)

    for is_sample in (False, True):
        @pl.when((k == pl.num_programs(1) - 1) & ((i >= npt) == is_sample))
        def _():
            x_copy(is_sample).wait()
            o_ref[...] = _layernorm(alpha * x_ref[...] + o_ref[...], g_ref[...], b_ref[...])


def _wo_ln(oa_p, ob_p, oa_s, ob_s, w_o, x_p, x_s, g, b, alpha):
    tp, d = x_p.shape
    ts = x_s.shape[0]
    wa = oa_p.shape[1]
    tm = _pick(math.gcd(tp, ts), (512, 256, 128))
    tk = _pick(wa, (512, 256, 128))
    npt, nst = tp // tm, ts // tm
    nka = wa // tk
    nkb = ob_p.shape[1] // tk

    def lhs_spec(is_sample, is_b):
        def index(i, k):
            row = jnp.clip(i - npt, 0, nst - 1) if is_sample else jnp.minimum(i, npt - 1)
            col = jnp.clip(k - nka, 0, nkb - 1) if is_b else jnp.minimum(k, nka - 1)
            return (row, jnp.where((i >= npt) == is_sample, col, 0))
        return pl.BlockSpec((tm, tk), index)

    return pl.pallas_call(
        functools.partial(_wo_ln_kernel, npt=npt, nka=nka, alpha=alpha),
        grid=(npt + nst, nka + nkb),
        in_specs=[lhs_spec(False, False), lhs_spec(False, True), lhs_spec(True, False), lhs_spec(True, True),
                  pl.BlockSpec((tk, d), lambda i, k: (k, 0)),
                  pl.BlockSpec(memory_space=pl.ANY),
                  pl.BlockSpec(memory_space=pl.ANY),
                  pl.BlockSpec((1, d), lambda i, k: (0, 0)),
                  pl.BlockSpec((1, d), lambda i, k: (0, 0))],
        out_specs=pl.BlockSpec((tm, d), lambda i, k: (i, 0)),
        out_shape=jax.ShapeDtypeStruct((tp + ts, d), F32),
        scratch_shapes=[pltpu.VMEM((tm, d), F32), pltpu.SemaphoreType.DMA(())],
        compiler_params=_cparams(2),
        name="wo_ln1",
    )(oa_p, ob_p, oa_s, ob_s, w_o, x_p, x_s, g, b)


def _router_kernel(h_ref, whi_ref, wlo_ref, b_ref, oi_ref, of_ref):
    h = h_ref[...]
    hhi = h.astype(BF)
    hlo = (h - hhi.astype(F32)).astype(BF)
    whi = whi_ref[...]
    logits = (jnp.dot(hhi, whi, preferred_element_type=F32)
              + jnp.dot(hhi, wlo_ref[...], preferred_element_type=F32)
              + jnp.dot(hlo, whi, preferred_element_type=F32)) + b_ref[...]
    lane = lax.broadcasted_iota(I32, logits.shape, 1)
    big = jnp.int32(2 ** 30)
    is_g = lane < N_GROUPS
    lgp = jnp.where(is_g, logits, -jnp.inf)
    mg = jnp.max(lgp, axis=-1, keepdims=True)
    p_grp = 1.0 / jnp.sum(jnp.exp(lgp - mg), axis=-1, keepdims=True)
    grp = jnp.min(jnp.where(lgp == mg, lane, big), axis=-1, keepdims=True)
    lo = N_GROUPS + grp * EXPERTS_PER_GROUP
    in_g = (lane >= lo) & (lane < lo + EXPERTS_PER_GROUP)
    le = jnp.where(in_g, logits, -jnp.inf)
    m1 = jnp.max(le, axis=-1, keepdims=True)
    i1 = jnp.min(jnp.where(le == m1, lane, big), axis=-1, keepdims=True)
    le2 = jnp.where(lane == i1, -jnp.inf, le)
    m2 = jnp.max(le2, axis=-1, keepdims=True)
    i2 = jnp.min(jnp.where(le2 == m2, lane, big), axis=-1, keepdims=True)
    e2 = jnp.exp(m2 - m1)
    g1 = p_grp / (1.0 + e2)
    g2 = p_grp * e2 / (1.0 + e2)
    oi_ref[...] = jnp.where(lane == 0, i1 - N_GROUPS, jnp.where(lane == 1, i2 - N_GROUPS, 0))
    of_ref[...] = jnp.where(lane == 0, g1, jnp.where(lane == 1, g2, 0.0))


def _router(h, w_hi, w_lo, bias):
    tt, d = h.shape
    tm = _pick(tt, (512, 256, 128))
    return pl.pallas_call(
        _router_kernel,
        grid=(tt // tm,),
        in_specs=[pl.BlockSpec((tm, d), lambda i: (i, 0)),
                  pl.BlockSpec((d, LANE), lambda i: (0, 0)),
                  pl.BlockSpec((d, LANE), lambda i: (0, 0)),
                  pl.BlockSpec((1, LANE), lambda i: (0, 0))],
        out_specs=[pl.BlockSpec((tm, LANE), lambda i: (i, 0)), pl.BlockSpec((tm, LANE), lambda i: (i, 0))],
        out_shape=[jax.ShapeDtypeStruct((tt, LANE), I32), jax.ShapeDtypeStruct((tt, LANE), F32)],
        compiler_params=_cparams(1),
        name="moe_router",
    )(h, w_hi, w_lo, bias)


MOE_BM = 384


MOE_DMA_SPLIT = 8
MOE_SLOTS = 3


def _experts_kernel(bexp_ref, brows_ref, bsrc_ref, tok_ref, h_hbm, wg_hbm, wu_hbm, wd_hbm, o_ref,
                    xf_ref, xb_ref, g_ref, u_ref, hb_ref, wgu_ref, wdn_ref, sem_x, sem_gu, sem_dn,
                    *, bm, nka, nkb, tka, tn):
    j = pl.program_id(0)
    k = pl.program_id(1)
    nblk = pl.num_programs(0)
    nsteps = nka + nkb
    na = tok_ref.shape[0]
    de = wdn_ref.shape[1]

    part = bm // nka

    def gather_start(blk, lo, n):
        def body(i, carry):
            r = lo + i
            tok = tok_ref[jnp.minimum(bsrc_ref[blk] + r, na - 1)]
            pltpu.make_async_copy(h_hbm.at[pl.ds(tok, 1)], xf_ref.at[pl.ds(r, 1)], sem_x).start()
            return carry

        lax.fori_loop(0, n, body, 0, unroll=8)

    def gather_wait():
        pltpu.make_async_copy(h_hbm.at[pl.ds(0, bm)], xf_ref, sem_x).wait()

    def w_tile(blk, step, start):
        step = jnp.asarray(step, I32)
        e = bexp_ref[blk]

        def run(cp):
            if start:
                cp.start()
            else:
                cp.wait()

        @pl.when(step < nka)
        def _():
            slot = lax.rem(step, MOE_SLOTS)
            rq = tka // MOE_DMA_SPLIT
            for q in range(MOE_DMA_SPLIT):
                rows = pl.ds(pl.multiple_of(step * tka + q * rq, rq), rq)
                for which, w_hbm in enumerate((wg_hbm, wu_hbm)):
                    run(pltpu.make_async_copy(w_hbm.at[0, e, rows, :],
                                              wgu_ref.at[slot, which, pl.ds(q * rq, rq), :], sem_gu.at[slot]))

        @pl.when(step >= nka)
        def _():
            f = step - nka
            slot = lax.rem(f, MOE_SLOTS)
            rq = de // MOE_DMA_SPLIT
            cols = pl.ds(pl.multiple_of(f * tn, tn), tn)
            for q in range(MOE_DMA_SPLIT):
                run(pltpu.make_async_copy(wd_hbm.at[0, e, pl.ds(q * rq, rq), cols],
                                          wdn_ref.at[slot, pl.ds(q * rq, rq), :], sem_dn.at[slot]))

    @pl.when((brows_ref[j] == 0) & (k == 0))
    def _():
        o_ref[...] = jnp.zeros(o_ref.shape, F32)

    @pl.when(brows_ref[j] > 0)
    def _():
        nxt = jnp.minimum(j + 1, nblk - 1)
        has_next = (j + 1 < nblk) & (brows_ref[nxt] > 0)

        @pl.when((j == 0) & (k == 0))
        def _():
            w_tile(j, 0, True)
            w_tile(j, 1, True)
            gather_start(j, 0, bm)

        @pl.when(k + 2 < nsteps)
        def _():
            w_tile(j, k + 2, True)

        @pl.when((k + 2 >= nsteps) & has_next)
        def _():
            w_tile(nxt, k + 2 - nsteps, True)

        @pl.when(k == 0)
        def _():
            gather_wait()
            xb_ref[...] = xf_ref[...].astype(BF)
            g_ref[...] = jnp.zeros(g_ref.shape, F32)
            u_ref[...] = jnp.zeros(u_ref.shape, F32)

        @pl.when((k < nka) & has_next)
        def _():
            gather_start(nxt, k * part, part)

        w_tile(j, k, False)

        @pl.when(k < nka)
        def _():
            slot = lax.rem(k, MOE_SLOTS)
            xk = xb_ref[:, pl.ds(pl.multiple_of(k * tka, tka), tka)]
            g_ref[...] += jnp.dot(xk, wgu_ref[slot, 0].astype(BF), preferred_element_type=F32)
            u_ref[...] += jnp.dot(xk, wgu_ref[slot, 1].astype(BF), preferred_element_type=F32)

        @pl.when(k == nka)
        def _():
            g = g_ref[...]
            hb_ref[...] = (g * jax.nn.sigmoid(g) * u_ref[...]).astype(BF)

        for f in range(nkb):
            @pl.when(k == nka + f)
            def _():
                o_ref[:, f * tn:(f + 1) * tn] = jnp.dot(hb_ref[...], wdn_ref[f % MOE_SLOTS].astype(BF),
                                                        preferred_element_type=F32)


def _experts(plan, h, w_gate, w_up, w_down, bm, nblk):
    tt, d = h.shape
    de = w_gate.shape[-1]
    tka = min(512, d // 2)
    tn = min(1024, d // 2)
    nka = d // tka
    nkb = d // tn
    assert tka % MOE_DMA_SPLIT == 0 and de % MOE_DMA_SPLIT == 0 and bm % nka == 0
    assert d % tka == 0 and d % tn == 0 and nka >= 2 and nkb >= 2
    hbm = pl.BlockSpec(memory_space=pl.ANY)
    grid_spec = pltpu.PrefetchScalarGridSpec(
        num_scalar_prefetch=4,
        grid=(nblk, nka + nkb),
        in_specs=[hbm, hbm, hbm, hbm],
        out_specs=pl.BlockSpec((bm, d), lambda j, k, be, br, bs, tk: (j, 0)),
        scratch_shapes=[pltpu.VMEM((bm, d), F32), pltpu.VMEM((bm, d), BF),
                        pltpu.VMEM((bm, de), F32), pltpu.VMEM((bm, de), F32), pltpu.VMEM((bm, de), BF),
                        pltpu.VMEM((MOE_SLOTS, 2, tka, de), F32), pltpu.VMEM((MOE_SLOTS, de, tn), F32),
                        pltpu.SemaphoreType.DMA(()), pltpu.SemaphoreType.DMA((MOE_SLOTS,)), pltpu.SemaphoreType.DMA((MOE_SLOTS,))],
    )
    return pl.pallas_call(
        functools.partial(_experts_kernel, bm=bm, nka=nka, nkb=nkb, tka=tka, tn=tn),
        grid_spec=grid_spec,
        out_shape=jax.ShapeDtypeStruct((nblk * bm, d), F32),
        compiler_params=_cparams(2),
        name="moe_experts",
    )(plan["b_exp"], plan["b_rows"], plan["b_src"], plan["tok_sorted"], h, w_gate, w_up, w_down)


def _moe_plan(e1, e2, bm, nblk):
    tt = e1.shape[0]
    na = 2 * tt
    ids = jnp.arange(N_EXPERTS, dtype=I32)[None, :]

    def lookup(table, idx):
        return jnp.sum(jnp.where(idx[:, None] == ids, table[None, :], 0), axis=1)

    e_flat = jnp.stack([e1, e2], axis=1).reshape(na)
    iota = jnp.arange(na, dtype=I32)
    e_sorted, order = lax.sort((e_flat, iota), num_keys=1, is_stable=True)
    counts = jnp.sum((e_flat[:, None] == ids).astype(I32), axis=0)
    nb_e = (counts + bm - 1) // bm
    blk_end = jnp.cumsum(nb_e)
    blk_start = blk_end - nb_e
    start = jnp.cumsum(counts) - counts
    dest = lookup(blk_start, e_sorted) * bm + iota - lookup(start, e_sorted)
    _, pos = lax.sort((order, dest), num_keys=1)
    j = jnp.arange(nblk, dtype=I32)
    n_valid = blk_end[-1]
    jj = jnp.minimum(j, n_valid - 1)
    b_exp = jnp.minimum(jnp.sum((blk_end[None, :] <= jj[:, None]).astype(I32), axis=1), N_EXPERTS - 1)
    first = (jj - lookup(blk_start, b_exp)) * bm
    b_rows = jnp.where(j < n_valid, jnp.clip(lookup(counts, b_exp) - first, 0, bm), 0)
    b_src = lookup(start, b_exp) + first
    return {"tok_sorted": order // 2, "pos": pos, "b_exp": b_exp, "b_rows": b_rows.astype(I32),
            "b_src": b_src.astype(I32)}


def _combine_ln_kernel(pos_ref, y_hbm, gate_ref, h_ref, g_ref, b_ref, op_ref, os_ref, r_ref, sem_ref,
                       *, tm, npt, alpha):
    i = pl.program_id(0)
    slot = lax.rem(i, 2)

    def gather_start(step, dst_slot):
        def body(r, carry):
            a = 2 * (step * tm + r)
            for pick in range(2):
                pltpu.make_async_copy(y_hbm.at[pl.ds(pos_ref[a + pick], 1)],
                                      r_ref.at[dst_slot, pick, pl.ds(r, 1)], sem_ref.at[dst_slot]).start()
            return carry

        lax.fori_loop(0, tm, body, 0, unroll=8)

    @pl.when(i == 0)
    def _():
        gather_start(i, slot)

    @pl.when(i + 1 < pl.num_programs(0))
    def _():
        gather_start(i + 1, 1 - slot)

    for pick in range(2):
        pltpu.make_async_copy(y_hbm.at[pl.ds(0, tm)], r_ref.at[slot, pick], sem_ref.at[slot]).wait()
    gate = gate_ref[...]

    y = gate[:, 0:1] * r_ref[slot, 0] + gate[:, 1:2] * r_ref[slot, 1]
    out = _layernorm(alpha * h_ref[...] + y, g_ref[...], b_ref[...])

    @pl.when(i < npt)
    def _():
        op_ref[...] = out

    @pl.when(i >= npt)
    def _():
        os_ref[...] = out


def _combine_ln(pos, y_sorted, gates, h, g, b, alpha, tp):
    tt, d = h.shape
    tm = _pick(math.gcd(tp, tt - tp), (128,))
    npt = tp // tm
    grid_spec = pltpu.PrefetchScalarGridSpec(
        num_scalar_prefetch=1,
        grid=(tt // tm,),
        in_specs=[pl.BlockSpec(memory_space=pl.ANY),
                  pl.BlockSpec((tm, LANE), lambda i, p: (i, 0)),
                  pl.BlockSpec((tm, d), lambda i, p: (i, 0)),
                  pl.BlockSpec((1, d), lambda i, p: (0, 0)),
                  pl.BlockSpec((1, d), lambda i, p: (0, 0))],
        out_specs=[pl.BlockSpec((tm, d), lambda i, p: (jnp.minimum(i, npt - 1), 0)),
                   pl.BlockSpec((tm, d), lambda i, p: (jnp.maximum(i - npt, 0), 0))],
        scratch_shapes=[pltpu.VMEM((2, 2, tm, d), F32), pltpu.SemaphoreType.DMA((2,))],
    )
    return pl.pallas_call(
        functools.partial(_combine_ln_kernel, tm=tm, npt=npt, alpha=alpha),
        grid_spec=grid_spec,
        out_shape=[jax.ShapeDtypeStruct((tp, d), F32), jax.ShapeDtypeStruct((tt - tp, d), F32)],
        compiler_params=_cparams(1),
        name="moe_combine_ln2",
    )(pos, y_sorted, gates, h, g, b)


def _rotate_half_cols(w):
    half = MLA_ROPE // 2
    return jnp.concatenate([-w[..., half:], w[..., :half]], axis=-1)


def _prep_w_in_t(w_in):
    wt = jnp.transpose(w_in)
    offs = [0]
    for s in (Q_RANK, KV_RANK, MLA_ROPE, DSA_HEADS * DSA_DIM, DSA_DIM, DSA_DIM, IDX_HEADS * IDX_DIM, IDX_DIM, IDX_HEADS):
        offs.append(offs[-1] + s)
    c_q, c_kv, k_pe, q_b, k_b, v_b, q_i, k_i, w_i = [wt[offs[n]:offs[n + 1]] for n in range(9)]
    half = MLA_ROPE // 2
    k_rot = jnp.concatenate([-k_pe[half:], k_pe[:half]], axis=0)
    pad = jnp.zeros((LANE - IDX_HEADS, wt.shape[1]), wt.dtype)
    return jnp.concatenate([q_i, q_b, c_kv, k_b, v_b, c_q, k_i, k_pe, k_rot, w_i, pad], axis=0).astype(BF)


def _lookup_rows(table, idx):
    onehot = (idx.reshape(-1, 1) == jnp.arange(table.shape[0], dtype=I32)[None, :]).astype(F32)
    out = jnp.dot(onehot, table.astype(F32), precision=lax.Precision.HIGHEST)
    return out.reshape(idx.shape + table.shape[1:])


def _prep_w_uq(w_uq):
    w = w_uq.reshape(Q_RANK, MLA_HEADS, MLA_NOPE + MLA_ROPE)
    w_nope = w[:, :, :MLA_NOPE].reshape(Q_RANK, MLA_HEADS * MLA_NOPE)
    pe = w[:, :, MLA_NOPE:]
    w_pr = jnp.concatenate([pe, _rotate_half_cols(pe)], axis=-1).reshape(Q_RANK, MLA_HEADS * 2 * MLA_ROPE)
    return w_nope.astype(BF), w_pr.astype(BF)


def _rope_table(pos):
    half = MLA_ROPE // 2
    inv = ROPE_THETA ** (-jnp.arange(half, dtype=F32) / half)
    ang = pos.astype(F32)[:, None] * inv[None, :]
    cos, sin = jnp.cos(ang), jnp.sin(ang)
    return jnp.concatenate([cos, cos, sin, sin], axis=-1)


def _bias_by_distance(rel_bias, n):
    dist = jnp.arange(n, dtype=I32)
    max_exact = NUM_BUCKETS // 2
    far = max_exact + (jnp.log(jnp.maximum(dist, 1).astype(F32) / max_exact)
                       / math.log(MAX_DISTANCE / max_exact) * (NUM_BUCKETS - max_exact)).astype(I32)
    bucket = jnp.where(dist < max_exact, dist, jnp.minimum(far, NUM_BUCKETS - 1))
    return rel_bias[bucket]


def kernel(x_prompt, x_sample, cache_latent, cache_kv, cache_idx, page_table, rel_bias, w_in, g_q, g_kv, w_uq, w_uk, w_uv, w_o, ln1_g, ln1_b, w_group, b_group, w_router, b_router, w_gate, w_up, w_down, ln2_g, ln2_b):
    depth = w_in.shape[0]
    assert depth == 1
    nbatch, seq, d_model = x_prompt.shape
    nseq, ntok, _ = x_sample.shape
    npages = page_table.shape[1]
    past = npages * PAGE_SIZE
    tp = nbatch * seq
    ts = nseq * ntok
    alpha = (2.0 * depth) ** 0.25
    assert seq % QBLK == 0 and ts % QBLK == 0 and QBLK % ntok == 0 and 4 <= ntok <= 8

    w_in_t = _prep_w_in_t(w_in[0])
    w_nope, w_pr = _prep_w_uq(w_uq[0])
    w_ukt = jnp.transpose(w_uk[0], (1, 2, 0)).astype(BF)
    w_uvt = jnp.transpose(w_uv[0], (1, 0, 2)).astype(BF)
    w_o_b = w_o[0].astype(BF)
    w_rt = jnp.concatenate([w_group[0], w_router[0],
                            jnp.zeros((d_model, LANE - N_GROUPS - N_EXPERTS), F32)], axis=1)
    w_rt_hi = w_rt.astype(BF)
    w_rt_lo = (w_rt - w_rt_hi.astype(F32)).astype(BF)
    b_rt = jnp.concatenate([b_group[0], b_router[0], jnp.zeros((LANE - N_GROUPS - N_EXPERTS,), F32)])[None, :]

    cs_p = _rope_table(jnp.tile(jnp.arange(seq, dtype=I32), nbatch))
    cs_s = _rope_table(jnp.tile(past + jnp.arange(ntok, dtype=I32), nseq))

    tab = _bias_by_distance(rel_bias, 2 * QBLK)
    far = rel_bias[NUM_BUCKETS - 1]
    tabc = (tab - far[None, :]).astype(F32)
    r = jnp.arange(QBLK, dtype=I32)[:, None]
    cc = jnp.arange(2 * QBLK, dtype=I32)[None, :]
    bt = jnp.transpose(_lookup_rows(tabc, jnp.clip(QBLK + r - cc, 0, 2 * QBLK - 1)), (2, 0, 1))
    tq = jnp.repeat(jnp.arange(ntok, dtype=I32), DSA_HEADS)[:, None]
    hq = jnp.tile(jnp.arange(DSA_HEADS, dtype=I32), ntok)[:, None, None]
    cl = jnp.arange(LANE, dtype=I32)[None, :]
    own_head = hq == jnp.arange(DSA_HEADS, dtype=I32)[None, None, :]

    def per_row_head(idx):
        return jnp.sum(jnp.where(own_head, _lookup_rows(tabc, idx), 0.0), axis=-1)

    b_last = per_row_head(jnp.clip(PAGE_SIZE + tq - cl, 0, 2 * QBLK - 1))
    b_new = jnp.where(cl <= tq, per_row_head(jnp.clip(tq - cl, 0, 2 * QBLK - 1)), 0.0)

    x_p = x_prompt.reshape(tp, d_model)
    x_s = x_sample.reshape(ts, d_model)
    hproj = _matmul(x_p, w_in_t, F32)
    hproj_s = _matmul(x_s, w_in_t, F32)
    q_cat_p = _qpath(hproj, g_q, w_nope, w_pr, w_ukt, cs_p)
    q_cat_s = _qpath(hproj_s, g_q, w_nope, w_pr, w_ukt, cs_s)
    lat_p, kvo_p, kio_p, lat_b, kv_b, ki_b = _kvpath(hproj, g_kv, cs_p)
    lat_s, kvo_s, kio_s, lat_bs, kv_bs, ki_bs = _kvpath(hproj_s, g_kv, cs_s)

    o_lat_p = _mla_prompt(q_cat_p, lat_b.reshape(nbatch, seq, LATENT_DIM))
    topk_p = min(IDX_TOPK_MAX, seq // 4)
    o_b_p = _dsa_prompt(hproj, ki_b.reshape(nbatch, seq, IDX_DIM), kv_b.reshape(nbatch, seq, 2 * DSA_DIM),
                        bt, nbatch, seq, topk_p)

    pg_lat = _pick(npages, (32, 16, 8, 4, 2, 1))
    pg_kv = _pick(npages, (32, 16, 8, 4, 2, 1))
    pg_idx = _pick(npages, (64, 32, 16, 8, 4, 2, 1))
    nbs = ts // QBLK
    spb = QBLK // ntok
    q_cat_s = q_cat_s.reshape(nbs, MLA_HEADS, spb, ntok, LATENT_DIM)
    q_cat_s = jnp.transpose(q_cat_s, (0, 2, 3, 1, 4)).reshape(nseq, ntok * MLA_HEADS, LATENT_DIM)

    def new_rows(a):
        a = a.reshape(nseq, ntok, a.shape[-1])
        return jnp.pad(a, ((0, 0), (0, LANE - ntok), (0, 0)))

    cache_latent_t = jnp.swapaxes(cache_latent, 2, 3)
    o_lat_s = _mla_sample(page_table, q_cat_s, cache_latent_t, new_rows(lat_bs), pg_lat)
    o_lat_s = o_lat_s.reshape(nbs, spb, ntok, MLA_HEADS, KV_RANK)
    o_lat_s = jnp.transpose(o_lat_s, (0, 3, 1, 2, 4)).reshape(nbs, MLA_HEADS, QBLK, KV_RANK)

    q_i_s = hproj_s[:, C_QI:C_QI + IDX_HEADS * IDX_DIM].astype(BF).reshape(nseq, ntok * IDX_HEADS, IDX_DIM)
    w_i_s = hproj_s[:, C_WI:C_WI + IDX_HEADS].reshape(nseq, ntok * IDX_HEADS, 1)
    s_past, s_new = _idx_sample(page_table, q_i_s, w_i_s, cache_idx, new_rows(ki_bs), pg_idx, ntok)
    topk_s = min(IDX_TOPK_MAX, (past + ntok) // 4)
    thr, cut = _topk_sample(s_past.reshape(ts, past), s_new.reshape(ts, LANE), topk_s)
    q_b_s = hproj_s[:, C_QB:C_QB + DSA_HEADS * DSA_DIM].astype(BF).reshape(nseq, ntok * DSA_HEADS, DSA_DIM)
    o_b_s = _dsa_sample(page_table, q_b_s, s_past, s_new, thr.reshape(nseq, ntok, LANE),
                        cut.reshape(nseq, ntok, LANE), cache_kv, new_rows(kv_bs), b_last, b_new, pg_kv, ntok)
    o_b_s = o_b_s.reshape(ts, DSA_HEADS * DSA_DIM)

    h1 = _wo_ln(_uv(o_lat_p, w_uvt), o_b_p, _uv(o_lat_s, w_uvt), o_b_s, w_o_b, x_p, x_s, ln1_g, ln1_b, alpha)

    r_idx, r_gate = _router(h1, w_rt_hi, w_rt_lo, b_rt)
    tt = tp + ts
    bm = MOE_BM
    nblk = (2 * tt) // bm + N_EXPERTS
    plan = _moe_plan(r_idx[:, 0], r_idx[:, 1], bm, nblk)
    y_sorted = _experts(plan, h1, w_gate, w_up, w_down, bm, nblk)
    out_p, out_s = _combine_ln(plan["pos"], y_sorted, r_gate, h1, ln2_g, ln2_b, alpha, tp)

    kvw = 2 * DSA_DIM
    return (out_p.reshape(nbatch, seq, d_model), out_s.reshape(nseq, ntok, d_model),
            lat_p.reshape(1, nbatch, seq, LATENT_DIM), kvo_p.reshape(1, nbatch, seq, kvw),
            kio_p.reshape(1, nbatch, seq, IDX_DIM),
            lat_s.reshape(1, nseq, ntok, LATENT_DIM), kvo_s.reshape(1, nseq, ntok, kvw),
            kio_s.reshape(1, nseq, ntok, IDX_DIM))
```
